```python
import jax, jax.numpy as jnp
from jax import lax
import numpy as np

D_MODEL = 1024
BATCH = 8
SEQ = 4096
DEPTH = 4

N_MIXERS = 2
N_GLA_LAYERS = (DEPTH + N_MIXERS - 1) // N_MIXERS
N_SWA_LAYERS = DEPTH // N_MIXERS
GLA_HEADS = 4
GLA_DK = D_MODEL // 2 // GLA_HEADS
GLA_DV = D_MODEL // GLA_HEADS
GLA_GATE_RANK = 16
GLA_GATE_TAU = 16.0
GLA_CHUNK = 64
GLA_IN = 2 * GLA_HEADS * GLA_DK + 2 * GLA_HEADS * GLA_DV + GLA_GATE_RANK
SWA_PATTERNS = ((128, 1), (512, 4), (2048, 16))
SWA_GROUPS = len(SWA_PATTERNS)
SWA_HEADS = 8
SWA_HEAD_DIM = D_MODEL // SWA_HEADS
SWA_IN = SWA_GROUPS * 3 * SWA_HEADS * SWA_HEAD_DIM
D_FF = 4 * D_MODEL
EPS = 1e-6

kernel_name = 'hybrid_gla_dilated_swa_sqrelu'


def _rmsnorm(x, g):
    x32 = x.astype(jnp.float32)
    y = x32 * lax.rsqrt(jnp.mean(x32 * x32, axis=-1, keepdims=True) + EPS)
    return (y * g.astype(jnp.float32)).astype(x.dtype)


def _gla_chunk_step(state, inp):
    q, k, v, g = inp
    C = q.shape[2]
    b = jnp.cumsum(g, axis=2)
    causal = jnp.tril(jnp.ones((C, C), dtype=bool))[:, :, None]
    diff = b[:, :, :, None, :] - b[:, :, None, :, :]
    decay = jnp.where(causal, jnp.exp(jnp.where(causal, diff, 0.0)), 0.0)
    attn = jnp.einsum('bhik,bhjk,bhijk->bhij', q, k, decay)
    o = (jnp.einsum('bhij,bhjv->bhiv', attn, v)
         + jnp.einsum('bhik,bhkv->bhiv', q * jnp.exp(b), state))
    b_last = b[:, :, -1, :]
    k_dec = k * jnp.exp(b_last[:, :, None, :] - b)
    new_state = state * jnp.exp(b_last)[..., None] + jnp.einsum('bhjk,bhjv->bhkv', k_dec, v)
    return new_state, o


def _gla_mixer(h, w_in, w_gate_up, b_gate, g_out, w_out):
    Bn, S, _ = h.shape
    hk = GLA_HEADS * GLA_DK
    hv = GLA_HEADS * GLA_DV
    proj = h @ w_in
    q, k, v, r, z = jnp.split(proj, [hk, 2 * hk, 2 * hk + hv, 2 * hk + 2 * hv], axis=-1)
    log_a = jax.nn.log_sigmoid((z @ w_gate_up + b_gate).astype(jnp.float32)) / GLA_GATE_TAU
    nc = S // GLA_CHUNK

    def to_chunks(t, d):
        return t.astype(jnp.float32).reshape(Bn, nc, GLA_CHUNK, GLA_HEADS, d).transpose(1, 0, 3, 2, 4)

    xs = (to_chunks(q * GLA_DK ** -0.5, GLA_DK), to_chunks(k, GLA_DK),
          to_chunks(v, GLA_DV), to_chunks(log_a, GLA_DK))
    state0 = jnp.zeros((Bn, GLA_HEADS, GLA_DK, GLA_DV), jnp.float32)
    _, o = lax.scan(_gla_chunk_step, state0, xs)
    o = o.transpose(1, 0, 3, 2, 4).reshape(Bn, S, GLA_HEADS, GLA_DV)
    o = _rmsnorm(o, g_out)
    o = o.reshape(Bn, S, hv).astype(h.dtype) * jax.nn.silu(r)
    return o @ w_out


def _dilated_band_attention(q, k, v, window, dilation):
    Bn, S, H, D = q.shape
    span = window // dilation
    blk = span
    L = S // dilation
    nb = -(-L // blk)
    Lp = nb * blk

    def to_sub(t):
        t = t.reshape(Bn, L, dilation, H, D).transpose(0, 2, 1, 3, 4).reshape(Bn * dilation, L, H, D)
        t = jnp.pad(t, ((0, 0), (0, Lp - L), (0, 0), (0, 0)))
        return t.reshape(Bn * dilation, nb, blk, H, D)

    def with_prev(t):
        prev = jnp.pad(t, ((0, 0), (1, 0), (0, 0), (0, 0), (0, 0)))[:, :-1]
        return jnp.concatenate([prev, t], axis=2)

    qs, ks, vs = to_sub(q), to_sub(k), to_sub(v)
    kk, vv = with_prev(ks), with_prev(vs)
    s = jnp.einsum('nbqhd,nbkhd->nbhqk', qs, kk).astype(jnp.float32)
    qpos = jnp.arange(nb)[:, None] * blk + jnp.arange(blk)[None, :]
    kpos = (jnp.arange(nb)[:, None] - 1) * blk + jnp.arange(2 * blk)[None, :]
    dist = qpos[:, :, None] - kpos[:, None, :]
    valid = (dist >= 0) & (dist <= span) & (kpos[:, None, :] >= 0)
    s = jnp.where(valid[None, :, None], s, -jnp.inf)
    m = jnp.max(s, axis=-1, keepdims=True)
    p = jnp.exp(s - m)
    l = jnp.sum(p, axis=-1, keepdims=True)
    o = jnp.einsum('nbhqk,nbkhd->nbhqd', p, vv.astype(jnp.float32)) / l
    lse = (m + jnp.log(l))[..., 0]
    o = o.transpose(0, 1, 3, 2, 4).reshape(Bn * dilation, Lp, H, D)[:, :L]
    o = o.reshape(Bn, dilation, L, H, D).transpose(0, 2, 1, 3, 4).reshape(Bn, S, H, D)
    lse = lse.transpose(0, 1, 3, 2).reshape(Bn * dilation, Lp, H)[:, :L]
    lse = lse.reshape(Bn, dilation, L, H).transpose(0, 2, 1, 3).reshape(Bn, S, H)
    return o, lse


def _dilated_mixer(h, w_qkv, g_q, g_k, w_out):
    Bn, S, _ = h.shape
    proj = (h @ w_qkv).reshape(Bn, S, SWA_GROUPS, 3, SWA_HEADS, SWA_HEAD_DIM)
    outs, lses = [], []
    for gi, (window, dilation) in enumerate(SWA_PATTERNS):
        q = _rmsnorm(proj[:, :, gi, 0], g_q[gi]) * SWA_HEAD_DIM ** -0.5
        k = _rmsnorm(proj[:, :, gi, 1], g_k[gi])
        o, lse = _dilated_band_attention(q, k, proj[:, :, gi, 2], window, dilation)
        outs.append(o)
        lses.append(lse)
    wts = jax.nn.softmax(jnp.stack(lses), axis=0)
    o = jnp.sum(wts[..., None] * jnp.stack(outs), axis=0)
    return o.reshape(Bn, S, SWA_HEADS * SWA_HEAD_DIM).astype(h.dtype) @ w_out


def _sqrelu_mlp(h, w_up, w_down):
    return jnp.square(jax.nn.relu(h @ w_up)) @ w_down


def _normal(key, shape, scale):
    return jax.random.normal(key, shape, jnp.float32) * scale


def setup_inputs(seed: int = 0) -> dict:
    key = jax.random.key(seed)
    ks = jax.random.split(key, 16)
    hk = GLA_HEADS * GLA_DK
    hv = GLA_HEADS * GLA_DV
    sh = SWA_HEADS * SWA_HEAD_DIM
    return {
        'x': _normal(ks[0], (BATCH, SEQ, D_MODEL), 1.0),
        'norm_mix': 1.0 + _normal(ks[1], (DEPTH, D_MODEL), 0.02),
        'norm_mlp': 1.0 + _normal(ks[2], (DEPTH, D_MODEL), 0.02),
        'gla_w_in': _normal(ks[3], (N_GLA_LAYERS, D_MODEL, GLA_IN), D_MODEL ** -0.5),
        'gla_w_gate_up': _normal(ks[4], (N_GLA_LAYERS, GLA_GATE_RANK, hk), GLA_GATE_RANK ** -0.5),
        'gla_b_gate': _normal(ks[5], (N_GLA_LAYERS, hk), 0.01),
        'gla_g_out': 1.0 + _normal(ks[6], (N_GLA_LAYERS, GLA_HEADS, GLA_DV), 0.02),
        'gla_w_out': _normal(ks[7], (N_GLA_LAYERS, hv, D_MODEL), hv ** -0.5),
        'swa_w_qkv': _normal(ks[8], (N_SWA_LAYERS, D_MODEL, SWA_IN), D_MODEL ** -0.5),
        'swa_g_q': 1.0 + _normal(ks[9], (N_SWA_LAYERS, SWA_GROUPS, SWA_HEAD_DIM), 0.02),
        'swa_g_k': 1.0 + _normal(ks[10], (N_SWA_LAYERS, SWA_GROUPS, SWA_HEAD_DIM), 0.02),
        'swa_w_out': _normal(ks[11], (N_SWA_LAYERS, sh, D_MODEL), sh ** -0.5),
        'mlp_w_up': _normal(ks[12], (DEPTH, D_MODEL, D_FF), D_MODEL ** -0.5),
        'mlp_w_down': _normal(ks[13], (DEPTH, D_FF, D_MODEL), D_FF ** -0.5),
    }


def reference(x, norm_mix, norm_mlp, gla_w_in, gla_w_gate_up, gla_b_gate, gla_g_out, gla_w_out,
              swa_w_qkv, swa_g_q, swa_g_k, swa_w_out, mlp_w_up, mlp_w_down):
    for i in range(DEPTH):
        j = i // N_MIXERS
        h = _rmsnorm(x, norm_mix[i])
        if i % N_MIXERS == 0:
            y = _gla_mixer(h, gla_w_in[j], gla_w_gate_up[j], gla_b_gate[j], gla_g_out[j], gla_w_out[j])
        else:
            y = _dilated_mixer(h, swa_w_qkv[j], swa_g_q[j], swa_g_k[j], swa_w_out[j])
        x = x + y
        x = x + _sqrelu_mlp(_rmsnorm(x, norm_mlp[i]), mlp_w_up[i], mlp_w_down[i])
    return x
```

```python
import functools

import jax
import jax.numpy as jnp
from jax import lax
from jax.experimental import pallas as pl
from jax.experimental.pallas import tpu as pltpu

EPS = 1e-6

GLA_HEADS = 4
GLA_DK = 128
GLA_DV = 256
GLA_GATE_RANK = 16
GLA_GATE_TAU = 16.0
GLA_CHUNK = 64
GLA_SUB = 16
GLA_BLOCK = 512

SWA_PATTERNS = ((128, 1), (512, 4), (2048, 16))
SWA_HEADS = 8
SWA_HEAD_DIM = 128

LANES = 128
VMEM_LIMIT_BYTES = 56 * 1024 * 1024


def _cparams(semantics):
    return pltpu.CompilerParams(dimension_semantics=semantics,
                                vmem_limit_bytes=VMEM_LIMIT_BYTES)


def _rms_scale(x32):
    return lax.rsqrt(jnp.mean(x32 * x32, axis=-1, keepdims=True) + EPS)


def _norm_matmul_kernel(x_ref, g_ref, w_ref, o_ref, h_scr):
    @pl.when(pl.program_id(1) == 0)
    def _():
        x = x_ref[...]
        h_scr[...] = (x * _rms_scale(x) * g_ref[...]).astype(jnp.bfloat16)

    o_ref[...] = jnp.dot(h_scr[...], w_ref[...],
                         preferred_element_type=jnp.float32).astype(o_ref.dtype)


def _norm_matmul(x2d, g, w, *, tm, tn, out_dtype=jnp.bfloat16):
    m, d = x2d.shape
    n = w.shape[1]
    return pl.pallas_call(
        _norm_matmul_kernel,
        grid=(m // tm, n // tn),
        in_specs=[
            pl.BlockSpec((tm, d), lambda i, j: (i, 0)),
            pl.BlockSpec((1, d), lambda i, j: (0, 0)),
            pl.BlockSpec((d, tn), lambda i, j: (0, j)),
        ],
        out_specs=pl.BlockSpec((tm, tn), lambda i, j: (i, j)),
        out_shape=jax.ShapeDtypeStruct((m, n), out_dtype),
        scratch_shapes=[pltpu.VMEM((tm, d), jnp.bfloat16)],
        compiler_params=_cparams(("parallel", "arbitrary")),
        name="norm_matmul",
    )(x2d, g.reshape(1, d), w)


def _gla_in_kernel(x_ref, g_ref, w_ref, wz_ref, o_ref, z_ref, h_scr):
    @pl.when(pl.program_id(1) == 0)
    def _():
        x = x_ref[...]
        h = (x * _rms_scale(x) * g_ref[...]).astype(jnp.bfloat16)
        h_scr[...] = h
        z_ref[...] = jnp.dot(h, wz_ref[...], preferred_element_type=jnp.float32)

    o_ref[...] = jnp.dot(h_scr[...], w_ref[...],
                         preferred_element_type=jnp.float32).astype(o_ref.dtype)


def _gla_in_proj(x2d, g, w_main, w_z, *, tm, tn):
    m, d = x2d.shape
    n = w_main.shape[1]
    nz = w_z.shape[1]
    return pl.pallas_call(
        _gla_in_kernel,
        grid=(m // tm, n // tn),
        in_specs=[
            pl.BlockSpec((tm, d), lambda i, j: (i, 0)),
            pl.BlockSpec((1, d), lambda i, j: (0, 0)),
            pl.BlockSpec((d, tn), lambda i, j: (0, j)),
            pl.BlockSpec((d, nz), lambda i, j: (0, 0)),
        ],
        out_specs=[
            pl.BlockSpec((tm, tn), lambda i, j: (i, j)),
            pl.BlockSpec((tm, nz), lambda i, j: (i, 0)),
        ],
        out_shape=[
            jax.ShapeDtypeStruct((m, n), jnp.bfloat16),
            jax.ShapeDtypeStruct((m, nz), jnp.float32),
        ],
        scratch_shapes=[pltpu.VMEM((tm, d), jnp.bfloat16)],
        compiler_params=_cparams(("parallel", "arbitrary")),
        name="gla_in_proj",
    )(x2d, g.reshape(1, d), w_main, w_z)


def _matmul_residual_kernel(o_ref, w_ref, x_ref, out_ref):
    out_ref[...] = x_ref[...] + jnp.dot(o_ref[...], w_ref[...],
                                        preferred_element_type=jnp.float32)


def _matmul_residual(o2d, w, x2d, *, tm):
    m, d = x2d.shape
    k = o2d.shape[1]
    return pl.pallas_call(
        _matmul_residual_kernel,
        grid=(m // tm,),
        in_specs=[
            pl.BlockSpec((tm, k), lambda i: (i, 0)),
            pl.BlockSpec((k, d), lambda i: (0, 0)),
            pl.BlockSpec((tm, d), lambda i: (i, 0)),
        ],
        out_specs=pl.BlockSpec((tm, d), lambda i: (i, 0)),
        out_shape=jax.ShapeDtypeStruct((m, d), jnp.float32),
        compiler_params=_cparams(("parallel",)),
        name="matmul_residual",
    )(o2d, w, x2d)


def _mlp_kernel(x_ref, g_ref, wu_ref, wd_ref, out_ref, h_scr, acc_scr):
    f = pl.program_id(1)

    @pl.when(f == 0)
    def _():
        x = x_ref[...]
        h_scr[...] = (x * _rms_scale(x) * g_ref[...]).astype(jnp.bfloat16)
        acc_scr[...] = jnp.zeros_like(acc_scr)

    u = jnp.dot(h_scr[...], wu_ref[...], preferred_element_type=jnp.float32)
    a = jnp.square(jnp.maximum(u, 0.0)).astype(jnp.bfloat16)
    acc_scr[...] += jnp.dot(a, wd_ref[...], preferred_element_type=jnp.float32)

    @pl.when(f == pl.num_programs(1) - 1)
    def _():
        out_ref[...] = x_ref[...] + acc_scr[...]


def _mlp(x2d, g, w_up, w_down, *, tm, tf):
    m, d = x2d.shape
    ff = w_up.shape[1]
    return pl.pallas_call(
        _mlp_kernel,
        grid=(m // tm, ff // tf),
        in_specs=[
            pl.BlockSpec((tm, d), lambda i, f: (i, 0)),
            pl.BlockSpec((1, d), lambda i, f: (0, 0)),
            pl.BlockSpec((d, tf), lambda i, f: (0, f)),
            pl.BlockSpec((tf, d), lambda i, f: (f, 0)),
        ],
        out_specs=pl.BlockSpec((tm, d), lambda i, f: (i, 0)),
        out_shape=jax.ShapeDtypeStruct((m, d), jnp.float32),
        scratch_shapes=[pltpu.VMEM((tm, d), jnp.bfloat16),
                        pltpu.VMEM((tm, d), jnp.float32)],
        compiler_params=_cparams(("parallel", "arbitrary")),
        name="sqrelu_mlp",
    )(x2d, g.reshape(1, d), w_up, w_down)


def _dot_nt(a, b):
    return lax.dot_general(a, b, (((1,), (1,)), ((), ())),
                           preferred_element_type=jnp.float32)


def _dot_tn(a, b):
    return lax.dot_general(a, b, (((0,), (0,)), ((), ())),
                           preferred_element_type=jnp.float32)


def _gla_kernel(q_ref, k_ref, v_ref, r_ref, z_ref, wg_ref, bg_ref, go_ref,
                o_ref, state_scr, b_scr, k_scr):
    C, SUB = GLA_CHUNK, GLA_SUB
    nsub = C // SUB

    @pl.when(pl.program_id(1) == 0)
    def _():
        state_scr[...] = jnp.zeros_like(state_scr)

    row = lax.broadcasted_iota(jnp.int32, (C, C), 0)
    col = lax.broadcasted_iota(jnp.int32, (C, C), 1)
    tri = jnp.where(row >= col, 1.0, 0.0).astype(jnp.bfloat16)
    srow = lax.broadcasted_iota(jnp.int32, (SUB, C), 0)
    scol = lax.broadcasted_iota(jnp.int32, (SUB, C), 1)

    def chunk_body(ci, carry):
        t0 = pl.multiple_of(ci * C, C)
        zc = z_ref[0, pl.ds(t0, C), :].astype(jnp.bfloat16)
        for h in range(GLA_HEADS):
            ks = slice(h * GLA_DK, (h + 1) * GLA_DK)
            vs = slice(h * GLA_DV, (h + 1) * GLA_DV)
            gp = jnp.dot(zc, wg_ref[:, ks], preferred_element_type=jnp.float32) + bg_ref[:, ks]
            g = (jnp.minimum(gp, 0.0) - jnp.log1p(jnp.exp(-jnp.abs(gp)))) * (1.0 / GLA_GATE_TAU)
            g_hi = g.astype(jnp.bfloat16)
            g_lo = (g - g_hi.astype(jnp.float32)).astype(jnp.bfloat16)
            b = (jnp.dot(tri, g_hi, preferred_element_type=jnp.float32)
                 + jnp.dot(tri, g_lo, preferred_element_type=jnp.float32))

            q = q_ref[0, pl.ds(t0, C), ks].astype(jnp.float32) * (GLA_DK ** -0.5)
            k = k_ref[0, pl.ds(t0, C), ks].astype(jnp.float32)
            v = v_ref[0, pl.ds(t0, C), vs]
            b_scr[h] = b
            k_scr[h] = k

            refs = [jnp.zeros((1, GLA_DK), jnp.float32)]
            for i in range(1, nsub):
                refs.append(b_scr[h, i * SUB - 1:i * SUB, :])
            ref_rows = jnp.concatenate(
                [jnp.broadcast_to(r, (SUB, GLA_DK)) for r in refs], axis=0)
            qe = (q * jnp.exp(b - ref_rows)).astype(jnp.bfloat16)

            a_rows = []
            for i in range(nsub):
                r0 = i * SUB
                qi = q[r0:r0 + SUB]
                bi = b[r0:r0 + SUB]
                key = jnp.where(srow >= scol - r0, scol, -1)
                blk = jnp.zeros((SUB, C), jnp.float32)
                for jj in range(SUB):
                    j = r0 + jj
                    bj = b_scr[h, j:j + 1, :]
                    kj = k_scr[h, j:j + 1, :]
                    e = jnp.exp(jnp.minimum(bi - bj, 0.0))
                    s = jnp.sum(qi * e * kj, axis=-1, keepdims=True)
                    blk = jnp.where(key == j, s, blk)
                if i > 0:
                    ke = (k * jnp.exp(jnp.minimum(refs[i] - b, 0.0))).astype(jnp.bfloat16)
                    off = _dot_nt(qe[r0:r0 + SUB], ke)
                    blk = jnp.where(scol < r0, off, blk)
                a_rows.append(blk)
            a = jnp.concatenate(a_rows, axis=0).astype(jnp.bfloat16)

            st = state_scr[h]
            q_in = (q * jnp.exp(b)).astype(jnp.bfloat16)
            o = (jnp.dot(a, v, preferred_element_type=jnp.float32)
                 + _dot_nt(q_in, st.astype(jnp.bfloat16)))

            b_last = b_scr[h, C - 1:C, :]
            k_dec = (k * jnp.exp(b_last - b)).astype(jnp.bfloat16)
            state_scr[h] = st * jnp.exp(b_last) + _dot_tn(v, k_dec)

            y = o * _rms_scale(o) * go_ref[:, vs]
            r = r_ref[0, pl.ds(t0, C), vs].astype(jnp.float32)
            y = y * (r / (1.0 + jnp.exp(-r)))
            o_ref[0, pl.ds(t0, C), vs] = y.astype(o_ref.dtype)
        return carry

    lax.fori_loop(0, GLA_BLOCK // C, chunk_body, 0)


def _gla_core(qkvr, z, w_gate, b_gate, g_out, *, batch, seq):
    hk = GLA_HEADS * GLA_DK
    hv = GLA_HEADS * GLA_DV
    tb = GLA_BLOCK
    tok = lambda col: (lambda b, t: (b, t, col))
    return pl.pallas_call(
        _gla_kernel,
        grid=(batch, seq // tb),
        in_specs=[
            pl.BlockSpec((1, tb, hk), tok(0)),
            pl.BlockSpec((1, tb, hk), tok(1)),
            pl.BlockSpec((1, tb, hv), tok(1)),
            pl.BlockSpec((1, tb, hv), tok(2)),
            pl.BlockSpec((1, tb, LANES), tok(0)),
            pl.BlockSpec((LANES, hk), lambda b, t: (0, 0)),
            pl.BlockSpec((1, hk), lambda b, t: (0, 0)),
            pl.BlockSpec((1, hv), lambda b, t: (0, 0)),
        ],
        out_specs=pl.BlockSpec((1, tb, hv), tok(0)),
        out_shape=jax.ShapeDtypeStruct((batch, seq, hv), jnp.bfloat16),
        scratch_shapes=[
            pltpu.VMEM((GLA_HEADS, GLA_DV, GLA_DK), jnp.float32),
            pltpu.VMEM((GLA_HEADS, GLA_CHUNK, GLA_DK), jnp.float32),
            pltpu.VMEM((GLA_HEADS, GLA_CHUNK, GLA_DK), jnp.float32),
        ],
        compiler_params=_cparams(("parallel", "arbitrary")),
        name="gla_core",
    )(qkvr, qkvr, qkvr, qkvr, z, w_gate, b_gate, g_out)


def _swa_kernel(q_ref, kp_ref, kc_ref, vp_ref, vc_ref, gq_ref, gk_ref,
                o_ref, lse_ref, *, blk):
    first = pl.program_id(2) == 0
    qi = lax.broadcasted_iota(jnp.int32, (blk, 2 * blk), 0)
    kc = lax.broadcasted_iota(jnp.int32, (blk, 2 * blk), 1)
    lo = jnp.where(first, blk, 0)
    valid = (kc >= jnp.maximum(qi, lo)) & (kc <= qi + blk)
    lane = lax.broadcasted_iota(jnp.int32, (blk, LANES), 1)
    gq = gq_ref[...] * (SWA_HEAD_DIM ** -0.5)
    gk = gk_ref[...]
    lse_tile = jnp.zeros((blk, LANES), jnp.float32)
    for h in range(SWA_HEADS):
        hs = slice(h * SWA_HEAD_DIM, (h + 1) * SWA_HEAD_DIM)
        q = q_ref[0, :, hs].astype(jnp.float32)
        q = (q * _rms_scale(q) * gq).astype(jnp.bfloat16)
        k = jnp.concatenate([kp_ref[0, :, hs], kc_ref[0, :, hs]], axis=0).astype(jnp.float32)
        k = (k * _rms_scale(k) * gk).astype(jnp.bfloat16)
        v = jnp.concatenate([vp_ref[0, :, hs], vc_ref[0, :, hs]], axis=0)
        s = jnp.where(valid, _dot_nt(q, k), -jnp.inf)
        m = jnp.max(s, axis=-1, keepdims=True)
        p = jnp.exp(s - m)
        l = jnp.sum(p, axis=-1, keepdims=True)
        o = jnp.dot(p.astype(jnp.bfloat16), v, preferred_element_type=jnp.float32) / l
        o_ref[0, :, hs] = o.astype(o_ref.dtype)
        lse_tile = jnp.where(lane == h, m + jnp.log(l), lse_tile)
    lse_ref[0] = lse_tile


def _swa_group(qkv, g_q, g_k, *, batch, seq, group, window, dilation):
    blk = window // dilation
    hd = SWA_HEADS * SWA_HEAD_DIM
    sub_len = seq // dilation
    nb = sub_len // blk
    ncol = qkv.shape[-1] // hd
    qkv_v = qkv.reshape(batch, sub_len, dilation * qkv.shape[-1])
    cq, ck, cv = group * 3, group * 3 + 1, group * 3 + 2
    cur = lambda c: (lambda b, r, j: (b, j, r * ncol + c))
    prev = lambda c: (lambda b, r, j: (b, jnp.maximum(j - 1, 0), r * ncol + c))
    o, lse = pl.pallas_call(
        functools.partial(_swa_kernel, blk=blk),
        grid=(batch, dilation, nb),
        in_specs=[
            pl.BlockSpec((1, blk, hd), cur(cq)),
            pl.BlockSpec((1, blk, hd), prev(ck)),
            pl.BlockSpec((1, blk, hd), cur(ck)),
            pl.BlockSpec((1, blk, hd), prev(cv)),
            pl.BlockSpec((1, blk, hd), cur(cv)),
            pl.BlockSpec((1, SWA_HEAD_DIM), lambda b, r, j: (0, 0)),
            pl.BlockSpec((1, SWA_HEAD_DIM), lambda b, r, j: (0, 0)),
        ],
        out_specs=[
            pl.BlockSpec((1, blk, hd), lambda b, r, j: (b, j, r)),
            pl.BlockSpec((1, blk, LANES), lambda b, r, j: (b, j, r)),
        ],
        out_shape=[
            jax.ShapeDtypeStruct((batch, sub_len, dilation * hd), jnp.bfloat16),
            jax.ShapeDtypeStruct((batch, sub_len, dilation * LANES), jnp.float32),
        ],
        compiler_params=_cparams(("parallel", "parallel", "arbitrary")),
        name=f"swa_group{group}",
    )(qkv_v, qkv_v, qkv_v, qkv_v, qkv_v, g_q.reshape(1, -1), g_k.reshape(1, -1))
    return o.reshape(batch * seq, hd), lse.reshape(batch * seq, LANES)


def _swa_out_kernel(o1_ref, o2_ref, o3_ref, l1_ref, l2_ref, l3_ref, w_ref, x_ref, out_ref):
    l1, l2, l3 = l1_ref[...], l2_ref[...], l3_ref[...]
    mx = jnp.maximum(jnp.maximum(l1, l2), l3)
    e1, e2, e3 = jnp.exp(l1 - mx), jnp.exp(l2 - mx), jnp.exp(l3 - mx)
    inv = 1.0 / (e1 + e2 + e3)
    w1, w2, w3 = e1 * inv, e2 * inv, e3 * inv
    parts = []
    for h in range(SWA_HEADS):
        hs = slice(h * SWA_HEAD_DIM, (h + 1) * SWA_HEAD_DIM)
        parts.append((w1[:, h:h + 1] * o1_ref[:, hs].astype(jnp.float32)
                      + w2[:, h:h + 1] * o2_ref[:, hs].astype(jnp.float32)
                      + w3[:, h:h + 1] * o3_ref[:, hs].astype(jnp.float32)).astype(jnp.bfloat16))
    o = jnp.concatenate(parts, axis=-1)
    out_ref[...] = x_ref[...] + jnp.dot(o, w_ref[...], preferred_element_type=jnp.float32)


def _swa_out(outs, lses, w, x2d, *, tm):
    m, d = x2d.shape
    hd = w.shape[0]
    tile = lambda n: pl.BlockSpec((tm, n), lambda i: (i, 0))
    return pl.pallas_call(
        _swa_out_kernel,
        grid=(m // tm,),
        in_specs=[tile(hd), tile(hd), tile(hd), tile(LANES), tile(LANES), tile(LANES),
                  pl.BlockSpec((hd, d), lambda i: (0, 0)), tile(d)],
        out_specs=tile(d),
        out_shape=jax.ShapeDtypeStruct((m, d), jnp.float32),
        compiler_params=_cparams(("parallel",)),
        name="swa_merge_out",
    )(*outs, *lses, w, x2d)


def kernel(x, norm_mix, norm_mlp, gla_w_in, gla_w_gate_up, gla_b_gate, gla_g_out, gla_w_out,
           swa_w_qkv, swa_g_q, swa_g_k, swa_w_out, mlp_w_up, mlp_w_down):
    batch, seq, d = x.shape
    depth = norm_mix.shape[0]
    hk = GLA_HEADS * GLA_DK
    hv = GLA_HEADS * GLA_DV
    n_main = 2 * hk + 2 * hv
    bf = jnp.bfloat16
    assert seq % GLA_BLOCK == 0
    for window, dilation in SWA_PATTERNS:
        assert seq % window == 0

    x2d = x.reshape(batch * seq, d)
    for i in range(depth):
        j = i // 2
        if i % 2 == 0:
            w_in = gla_w_in[j]
            w_z = jnp.pad(w_in[:, n_main:], ((0, 0), (0, LANES - GLA_GATE_RANK))).astype(bf)
            qkvr, z = _gla_in_proj(x2d, norm_mix[i], w_in[:, :n_main].astype(bf), w_z,
                                   tm=1024, tn=1024)
            w_gate = jnp.pad(gla_w_gate_up[j], ((0, LANES - GLA_GATE_RANK), (0, 0))).astype(bf)
            o = _gla_core(qkvr.reshape(batch, seq, n_main), z.reshape(batch, seq, LANES),
                          w_gate, gla_b_gate[j].reshape(1, hk), gla_g_out[j].reshape(1, hv),
                          batch=batch, seq=seq)
            x2d = _matmul_residual(o.reshape(batch * seq, hv), gla_w_out[j].astype(bf), x2d,
                                   tm=1024)
        else:
            qkv = _norm_matmul(x2d, norm_mix[i], swa_w_qkv[j].astype(bf), tm=1024, tn=1024)
            qkv = qkv.reshape(batch, seq, -1)
            outs, lses = [], []
            for gi, (window, dilation) in enumerate(SWA_PATTERNS):
                o, lse = _swa_group(qkv, swa_g_q[j, gi], swa_g_k[j, gi], batch=batch, seq=seq,
                                    group=gi, window=window, dilation=dilation)
                outs.append(o)
                lses.append(lse)
            x2d = _swa_out(outs, lses, swa_w_out[j].astype(bf), x2d, tm=512)
        x2d = _mlp(x2d, norm_mlp[i], mlp_w_up[i].astype(bf), mlp_w_down[i].astype(bf),
                   tm=1024, tf=1024)
    return x2d.reshape(batch, seq, d)
```

```python
import functools

import jax
import jax.numpy as jnp
from jax import lax
from jax.experimental import pallas as pl
from jax.experimental.pallas import tpu as pltpu

EPS = 1e-6

GLA_HEADS = 4
GLA_DK = 128
GLA_DV = 256
GLA_GATE_RANK = 16
GLA_GATE_TAU = 16.0
GLA_CHUNK = 64
GLA_SUB = 16
GLA_BLOCK = 512
GLA_FAST_MAX_DECAY = 40.0

SWA_PATTERNS = ((128, 1), (512, 4), (2048, 16))
SWA_HEADS = 8
SWA_HEAD_DIM = 128
SWA_SPAN = 128
PERM_ROWS = 256

LANES = 128
VMEM_LIMIT_BYTES = 56 * 1024 * 1024


def _cparams(semantics):
    return pltpu.CompilerParams(dimension_semantics=semantics,
                                vmem_limit_bytes=VMEM_LIMIT_BYTES)


def _rms_scale(x32):
    return lax.rsqrt(jnp.mean(x32 * x32, axis=-1, keepdims=True) + EPS)


def _perm_matrix(dil, to_natural):
    n = PERM_ROWS // dil
    row = lax.broadcasted_iota(jnp.int32, (PERM_ROWS, PERM_ROWS), 0)
    col = lax.broadcasted_iota(jnp.int32, (PERM_ROWS, PERM_ROWS), 1)
    tok, rm = (row, col) if to_natural else (col, row)
    rm_of_tok = (tok & (dil - 1)) * n + (tok >> (dil.bit_length() - 1))
    return jnp.where(rm == rm_of_tok, 1.0, 0.0).astype(jnp.bfloat16)


def _swa_proj_kernel(x_ref, g_ref, w_ref, gqk_ref, o_ref, h_scr, *, dil):
    j = pl.program_id(2)
    tm = x_ref.shape[1]
    n = PERM_ROWS // dil

    @pl.when(j == 0)
    def _():
        x = x_ref[0]
        h = (x * _rms_scale(x) * g_ref[...]).astype(jnp.bfloat16)
        if dil == 1:
            h_scr[...] = h
        else:
            perm = _perm_matrix(dil, to_natural=False)
            for sb in range(tm // PERM_ROWS):
                rows = slice(sb * PERM_ROWS, (sb + 1) * PERM_ROWS)
                h_scr[rows, :] = jnp.dot(perm, h[rows],
                                         preferred_element_type=jnp.float32).astype(jnp.bfloat16)

    def store(y_bf16, cols):
        if dil == 1:
            o_ref[0, 0, :, cols] = y_bf16
        else:
            for sb in range(tm // PERM_ROWS):
                for r in range(dil):
                    src = sb * PERM_ROWS + r * n
                    o_ref[0, r, sb * n:(sb + 1) * n, cols] = y_bf16[src:src + n]

    @pl.when(j < 2)
    def _():
        y = jnp.dot(h_scr[...], w_ref[...], preferred_element_type=jnp.float32)
        gain = gqk_ref[pl.ds(j, 1), :]
        for h in range(SWA_HEADS):
            hs = slice(h * SWA_HEAD_DIM, (h + 1) * SWA_HEAD_DIM)
            yh = y[:, hs]
            store((yh * _rms_scale(yh) * gain).astype(jnp.bfloat16), hs)

    @pl.when(j == 2)
    def _():
        y = jnp.dot(h_scr[...], w_ref[...], preferred_element_type=jnp.float32)
        store(y.astype(jnp.bfloat16), slice(None))


def _swa_proj(x3d, g, w_qkv, gains_qk, *, group, dil, tm):
    batch, seq, d = x3d.shape
    hd = SWA_HEADS * SWA_HEAD_DIM
    return pl.pallas_call(
        functools.partial(_swa_proj_kernel, dil=dil),
        grid=(batch, seq // tm, 3),
        in_specs=[
            pl.BlockSpec((1, tm, d), lambda b, i, j: (b, i, 0)),
            pl.BlockSpec((1, d), lambda b, i, j: (0, 0)),
            pl.BlockSpec((d, hd), lambda b, i, j: (0, group * 3 + j)),
            pl.BlockSpec((2, SWA_HEAD_DIM), lambda b, i, j: (0, 0)),
        ],
        out_specs=pl.BlockSpec((1, dil, tm // dil, hd), lambda b, i, j: (b, 0, i, j)),
        out_shape=jax.ShapeDtypeStruct((batch, dil, seq // dil, 3 * hd), jnp.bfloat16),
        scratch_shapes=[pltpu.VMEM((tm, d), jnp.bfloat16)],
        compiler_params=_cparams(("parallel", "parallel", "arbitrary")),
        name=f"swa_proj_d{dil}",
    )(x3d, g.reshape(1, d), w_qkv, gains_qk)


def _gla_in_kernel(x_ref, g_ref, w_ref, wz_ref, o_ref, z_ref, h_scr):
    @pl.when(pl.program_id(1) == 0)
    def _():
        x = x_ref[...]
        h = (x * _rms_scale(x) * g_ref[...]).astype(jnp.bfloat16)
        h_scr[...] = h
        z_ref[...] = jnp.dot(h, wz_ref[...], preferred_element_type=jnp.float32)

    o_ref[...] = jnp.dot(h_scr[...], w_ref[...],
                         preferred_element_type=jnp.float32).astype(o_ref.dtype)


def _gla_in_proj(x2d, g, w_main, w_z, *, tm, tn):
    m, d = x2d.shape
    n = w_main.shape[1]
    nz = w_z.shape[1]
    return pl.pallas_call(
        _gla_in_kernel,
        grid=(m // tm, n // tn),
        in_specs=[
            pl.BlockSpec((tm, d), lambda i, j: (i, 0)),
            pl.BlockSpec((1, d), lambda i, j: (0, 0)),
            pl.BlockSpec((d, tn), lambda i, j: (0, j)),
            pl.BlockSpec((d, nz), lambda i, j: (0, 0)),
        ],
        out_specs=[
            pl.BlockSpec((tm, tn), lambda i, j: (i, j)),
            pl.BlockSpec((tm, nz), lambda i, j: (i, 0)),
        ],
        out_shape=[
            jax.ShapeDtypeStruct((m, n), jnp.bfloat16),
            jax.ShapeDtypeStruct((m, nz), jnp.float32),
        ],
        scratch_shapes=[pltpu.VMEM((tm, d), jnp.bfloat16)],
        compiler_params=_cparams(("parallel", "arbitrary")),
        name="gla_in_proj",
    )(x2d, g.reshape(1, d), w_main, w_z)


def _matmul_residual_kernel(o_ref, w_ref, x_ref, out_ref):
    out_ref[...] = x_ref[...] + jnp.dot(o_ref[...], w_ref[...],
                                        preferred_element_type=jnp.float32)


def _matmul_residual(o2d, w, x2d, *, tm):
    m, d = x2d.shape
    k = o2d.shape[1]
    return pl.pallas_call(
        _matmul_residual_kernel,
        grid=(m // tm,),
        in_specs=[
            pl.BlockSpec((tm, k), lambda i: (i, 0)),
            pl.BlockSpec((k, d), lambda i: (0, 0)),
            pl.BlockSpec((tm, d), lambda i: (i, 0)),
        ],
        out_specs=pl.BlockSpec((tm, d), lambda i: (i, 0)),
        out_shape=jax.ShapeDtypeStruct((m, d), jnp.float32),
        compiler_params=_cparams(("parallel",)),
        name="matmul_residual",
    )(o2d, w, x2d)


def _mlp_kernel(x_ref, g_ref, wu_ref, wd_ref, out_ref, h_scr, acc_scr):
    f = pl.program_id(1)

    @pl.when(f == 0)
    def _():
        x = x_ref[...]
        h_scr[...] = (x * _rms_scale(x) * g_ref[...]).astype(jnp.bfloat16)
        acc_scr[...] = jnp.zeros_like(acc_scr)

    u = jnp.dot(h_scr[...], wu_ref[...], preferred_element_type=jnp.float32)
    a = jnp.square(jnp.maximum(u, 0.0)).astype(jnp.bfloat16)
    acc_scr[...] += jnp.dot(a, wd_ref[...], preferred_element_type=jnp.float32)

    @pl.when(f == pl.num_programs(1) - 1)
    def _():
        out_ref[...] = x_ref[...] + acc_scr[...]


def _mlp(x2d, g, w_up, w_down, *, tm, tf):
    m, d = x2d.shape
    ff = w_up.shape[1]
    return pl.pallas_call(
        _mlp_kernel,
        grid=(m // tm, ff // tf),
        in_specs=[
            pl.BlockSpec((tm, d), lambda i, f: (i, 0)),
            pl.BlockSpec((1, d), lambda i, f: (0, 0)),
            pl.BlockSpec((d, tf), lambda i, f: (0, f)),
            pl.BlockSpec((tf, d), lambda i, f: (f, 0)),
        ],
        out_specs=pl.BlockSpec((tm, d), lambda i, f: (i, 0)),
        out_shape=jax.ShapeDtypeStruct((m, d), jnp.float32),
        scratch_shapes=[pltpu.VMEM((tm, d), jnp.bfloat16),
                        pltpu.VMEM((tm, d), jnp.float32)],
        compiler_params=_cparams(("parallel", "arbitrary")),
        name="sqrelu_mlp",
    )(x2d, g.reshape(1, d), w_up, w_down)


def _dot_nt(a, b):
    return lax.dot_general(a, b, (((1,), (1,)), ((), ())),
                           preferred_element_type=jnp.float32)


def _dot_tn(a, b):
    return lax.dot_general(a, b, (((0,), (0,)), ((), ())),
                           preferred_element_type=jnp.float32)


def _gla_kernel(q_ref, k_ref, v_ref, r_ref, z_ref, wg_ref, bg_ref, go_ref,
                o_ref, state_scr, b_scr, k_scr):
    C, SUB = GLA_CHUNK, GLA_SUB
    nsub = C // SUB
    nchunk = GLA_BLOCK // C

    @pl.when(pl.program_id(1) == 0)
    def _():
        state_scr[...] = jnp.zeros_like(state_scr)

    row = lax.broadcasted_iota(jnp.int32, (C, C), 0)
    col = lax.broadcasted_iota(jnp.int32, (C, C), 1)
    causal = row >= col
    tri = jnp.where(causal, 1.0, 0.0).astype(jnp.bfloat16)

    zb = z_ref[0].astype(jnp.bfloat16)
    decay_max = jnp.zeros((1, GLA_DK), jnp.float32)
    for h in range(GLA_HEADS):
        ks = slice(h * GLA_DK, (h + 1) * GLA_DK)
        gp = jnp.dot(zb, wg_ref[:, ks], preferred_element_type=jnp.float32) + bg_ref[:, ks]
        g = (jnp.minimum(gp, 0.0) - jnp.log1p(jnp.exp(-jnp.abs(gp)))) * (1.0 / GLA_GATE_TAU)
        g_hi = g.astype(jnp.bfloat16)
        g_lo = (g - g_hi.astype(jnp.float32)).astype(jnp.bfloat16)
        for ci in range(nchunk):
            rows = slice(ci * C, (ci + 1) * C)
            b = (jnp.dot(tri, g_hi[rows], preferred_element_type=jnp.float32)
                 + jnp.dot(tri, g_lo[rows], preferred_element_type=jnp.float32))
            b_scr[h, rows, :] = b
            decay_max = jnp.maximum(decay_max, -b[C - 1:C, :])
    mild = jnp.max(decay_max) <= GLA_FAST_MAX_DECAY

    def finish(ci_rows, h, a, q_in, k, v, b, b_last):
        vs = slice(h * GLA_DV, (h + 1) * GLA_DV)
        st = state_scr[h]
        o = (jnp.dot(a, v, preferred_element_type=jnp.float32)
             + _dot_nt(q_in, st.astype(jnp.bfloat16)))
        k_dec = (k * jnp.exp(b_last - b)).astype(jnp.bfloat16)
        state_scr[h] = st * jnp.exp(b_last) + _dot_tn(v, k_dec)
        y = o * _rms_scale(o) * go_ref[:, vs]
        r = r_ref[0, ci_rows, vs].astype(jnp.float32)
        y = y * (r / (1.0 + jnp.exp(-r)))
        o_ref[0, ci_rows, vs] = y.astype(o_ref.dtype)

    @pl.when(mild)
    def _():
        for ci in range(nchunk):
            rows = slice(ci * C, (ci + 1) * C)
            for h in range(GLA_HEADS):
                ks = slice(h * GLA_DK, (h + 1) * GLA_DK)
                vs = slice(h * GLA_DV, (h + 1) * GLA_DV)
                b = b_scr[h, rows, :]
                q = q_ref[0, rows, ks].astype(jnp.float32) * (GLA_DK ** -0.5)
                k = k_ref[0, rows, ks].astype(jnp.float32)
                v = v_ref[0, rows, vs]
                q_in = (q * jnp.exp(b)).astype(jnp.bfloat16)
                k_out = (k * jnp.exp(-b)).astype(jnp.bfloat16)
                a = jnp.where(causal, _dot_nt(q_in, k_out), 0.0).astype(jnp.bfloat16)
                finish(rows, h, a, q_in, k, v, b, b[C - 1:C, :])

    @pl.when(jnp.logical_not(mild))
    def _():
        srow = lax.broadcasted_iota(jnp.int32, (SUB, C), 0)
        scol = lax.broadcasted_iota(jnp.int32, (SUB, C), 1)

        def chunk_body(ci, carry):
            rows = pl.ds(pl.multiple_of(ci * C, C), C)
            for h in range(GLA_HEADS):
                ks = slice(h * GLA_DK, (h + 1) * GLA_DK)
                vs = slice(h * GLA_DV, (h + 1) * GLA_DV)
                b = b_scr[h, rows, :]
                q = q_ref[0, rows, ks].astype(jnp.float32) * (GLA_DK ** -0.5)
                k = k_ref[0, rows, ks].astype(jnp.float32)
                v = v_ref[0, rows, vs]
                k_scr[h] = k
                c0 = ci * C

                refs = [jnp.zeros((1, GLA_DK), jnp.float32)]
                for i in range(1, nsub):
                    refs.append(b_scr[h, pl.ds(c0 + i * SUB - 1, 1), :])
                ref_rows = jnp.concatenate(
                    [jnp.broadcast_to(r, (SUB, GLA_DK)) for r in refs], axis=0)
                qe = (q * jnp.exp(b - ref_rows)).astype(jnp.bfloat16)

                a_rows = []
                for i in range(nsub):
                    r0 = i * SUB
                    qi = q[r0:r0 + SUB]
                    bi = b[r0:r0 + SUB]
                    key = jnp.where(srow >= scol - r0, scol, -1)
                    blk = jnp.zeros((SUB, C), jnp.float32)
                    for jj in range(SUB):
                        j = r0 + jj
                        bj = b_scr[h, pl.ds(c0 + j, 1), :]
                        kj = k_scr[h, j:j + 1, :]
                        e = jnp.exp(jnp.minimum(bi - bj, 0.0))
                        s = jnp.sum(qi * e * kj, axis=-1, keepdims=True)
                        blk = jnp.where(key == j, s, blk)
                    if i > 0:
                        ke = (k * jnp.exp(jnp.minimum(refs[i] - b, 0.0))).astype(jnp.bfloat16)
                        off = _dot_nt(qe[r0:r0 + SUB], ke)
                        blk = jnp.where(scol < r0, off, blk)
                    a_rows.append(blk)
                a = jnp.concatenate(a_rows, axis=0).astype(jnp.bfloat16)
                q_in = (q * jnp.exp(b)).astype(jnp.bfloat16)
                finish(rows, h, a, q_in, k, v, b, b[C - 1:C, :])
            return carry

        lax.fori_loop(0, nchunk, chunk_body, 0)


def _gla_core(qkvr, z, w_gate, b_gate, g_out, *, batch, seq):
    hk = GLA_HEADS * GLA_DK
    hv = GLA_HEADS * GLA_DV
    tb = GLA_BLOCK
    tok = lambda col: (lambda b, t: (b, t, col))
    return pl.pallas_call(
        _gla_kernel,
        grid=(batch, seq // tb),
        in_specs=[
            pl.BlockSpec((1, tb, hk), tok(0)),
            pl.BlockSpec((1, tb, hk), tok(1)),
            pl.BlockSpec((1, tb, hv), tok(1)),
            pl.BlockSpec((1, tb, hv), tok(2)),
            pl.BlockSpec((1, tb, LANES), tok(0)),
            pl.BlockSpec((LANES, hk), lambda b, t: (0, 0)),
            pl.BlockSpec((1, hk), lambda b, t: (0, 0)),
            pl.BlockSpec((1, hv), lambda b, t: (0, 0)),
        ],
        out_specs=pl.BlockSpec((1, tb, hv), tok(0)),
        out_shape=jax.ShapeDtypeStruct((batch, seq, hv), jnp.bfloat16),
        scratch_shapes=[
            pltpu.VMEM((GLA_HEADS, GLA_DV, GLA_DK), jnp.float32),
            pltpu.VMEM((GLA_HEADS, GLA_BLOCK, GLA_DK), jnp.float32),
            pltpu.VMEM((GLA_HEADS, GLA_CHUNK, GLA_DK), jnp.float32),
        ],
        compiler_params=_cparams(("parallel", "arbitrary")),
        name="gla_core",
    )(qkvr, qkvr, qkvr, qkvr, z, w_gate, b_gate, g_out)


def _swa_kernel(q_ref, k_ref, v_ref, o_ref, lse_ref, kprev_scr, vprev_scr, *, nsub):
    blk = SWA_SPAN
    first = pl.program_id(2) == 0

    @pl.when(first)
    def _():
        kprev_scr[...] = jnp.zeros_like(kprev_scr)
        vprev_scr[...] = jnp.zeros_like(vprev_scr)

    qi = lax.broadcasted_iota(jnp.int32, (blk, 2 * blk), 0)
    kc = lax.broadcasted_iota(jnp.int32, (blk, 2 * blk), 1)
    band = (kc >= qi) & (kc <= qi + blk)
    lo = jnp.where(first, blk, 0)
    band_first = (kc >= jnp.maximum(qi, lo)) & (kc <= qi + blk)
    lane = lax.broadcasted_iota(jnp.int32, (blk, LANES), 1)
    for s in range(nsub):
        rows = slice(s * blk, (s + 1) * blk)
        valid = band_first if s == 0 else band
        lse_tile = jnp.zeros((blk, LANES), jnp.float32)
        for h in range(SWA_HEADS):
            hs = slice(h * SWA_HEAD_DIM, (h + 1) * SWA_HEAD_DIM)
            q = q_ref[0, 0, rows, hs]
            if s == 0:
                k = jnp.concatenate([kprev_scr[:, hs], k_ref[0, 0, rows, hs]], axis=0)
                v = jnp.concatenate([vprev_scr[:, hs], v_ref[0, 0, rows, hs]], axis=0)
            else:
                k = k_ref[0, 0, (s - 1) * blk:(s + 1) * blk, hs]
                v = v_ref[0, 0, (s - 1) * blk:(s + 1) * blk, hs]
            sc = jnp.where(valid, _dot_nt(q, k), -jnp.inf)
            m = jnp.max(sc, axis=-1, keepdims=True)
            p = jnp.exp(sc - m)
            l = jnp.sum(p, axis=-1, keepdims=True)
            o = jnp.dot(p.astype(jnp.bfloat16), v, preferred_element_type=jnp.float32) / l
            o_ref[0, 0, rows, hs] = o.astype(o_ref.dtype)
            lse_tile = jnp.where(lane == h, m + jnp.log(l), lse_tile)
        lse_ref[0, 0, rows, :] = lse_tile
    kprev_scr[...] = k_ref[0, 0, (nsub - 1) * blk:nsub * blk, :]
    vprev_scr[...] = v_ref[0, 0, (nsub - 1) * blk:nsub * blk, :]


def _swa_group(qkv, *, dil, rows):
    batch, _, sub_len, _ = qkv.shape
    hd = SWA_HEADS * SWA_HEAD_DIM
    col = lambda c: (lambda b, r, j: (b, r, j, c))
    return pl.pallas_call(
        functools.partial(_swa_kernel, nsub=rows // SWA_SPAN),
        grid=(batch, dil, sub_len // rows),
        in_specs=[pl.BlockSpec((1, 1, rows, hd), col(0)),
                  pl.BlockSpec((1, 1, rows, hd), col(1)),
                  pl.BlockSpec((1, 1, rows, hd), col(2))],
        out_specs=[pl.BlockSpec((1, 1, rows, hd), col(0)),
                   pl.BlockSpec((1, 1, rows, LANES), col(0))],
        out_shape=[jax.ShapeDtypeStruct((batch, dil, sub_len, hd), jnp.bfloat16),
                   jax.ShapeDtypeStruct((batch, dil, sub_len, LANES), jnp.float32)],
        scratch_shapes=[pltpu.VMEM((SWA_SPAN, hd), jnp.bfloat16),
                        pltpu.VMEM((SWA_SPAN, hd), jnp.bfloat16)],
        compiler_params=_cparams(("parallel", "parallel", "arbitrary")),
        name=f"swa_attn_d{dil}",
    )(qkv, qkv, qkv)


def _swa_out_kernel(o1_ref, o2_ref, o3_ref, l1_ref, l2_ref, l3_ref, w_ref, x_ref, out_ref):
    def natural_bf16(ref, dil):
        rm = ref[0].reshape(PERM_ROWS, ref.shape[-1])
        if dil == 1:
            return rm.astype(jnp.float32)
        return jnp.dot(_perm_matrix(dil, to_natural=True), rm,
                       preferred_element_type=jnp.float32)

    def natural_f32(ref, dil):
        rm = ref[0].reshape(PERM_ROWS, ref.shape[-1])
        if dil == 1:
            return rm
        perm = _perm_matrix(dil, to_natural=True)
        hi = rm.astype(jnp.bfloat16)
        r1 = rm - hi.astype(jnp.float32)
        mid = r1.astype(jnp.bfloat16)
        lo = (r1 - mid.astype(jnp.float32)).astype(jnp.bfloat16)
        return (jnp.dot(perm, hi, preferred_element_type=jnp.float32)
                + jnp.dot(perm, mid, preferred_element_type=jnp.float32)
                + jnp.dot(perm, lo, preferred_element_type=jnp.float32))

    dils = [d for _, d in SWA_PATTERNS]
    l1, l2, l3 = (natural_f32(r, d) for r, d in zip((l1_ref, l2_ref, l3_ref), dils))
    o1, o2, o3 = (natural_bf16(r, d) for r, d in zip((o1_ref, o2_ref, o3_ref), dils))
    mx = jnp.maximum(jnp.maximum(l1, l2), l3)
    e1, e2, e3 = jnp.exp(l1 - mx), jnp.exp(l2 - mx), jnp.exp(l3 - mx)
    inv = 1.0 / (e1 + e2 + e3)
    w1, w2, w3 = e1 * inv, e2 * inv, e3 * inv
    parts = []
    for h in range(SWA_HEADS):
        hs = slice(h * SWA_HEAD_DIM, (h + 1) * SWA_HEAD_DIM)
        parts.append((w1[:, h:h + 1] * o1[:, hs] + w2[:, h:h + 1] * o2[:, hs]
                      + w3[:, h:h + 1] * o3[:, hs]).astype(jnp.bfloat16))
    o = jnp.concatenate(parts, axis=-1)
    out_ref[0] = x_ref[0] + jnp.dot(o, w_ref[...], preferred_element_type=jnp.float32)


def _swa_out(outs, lses, w, x3d):
    batch, seq, d = x3d.shape
    hd = w.shape[0]
    tm = PERM_ROWS

    def group_tile(dil, n):
        return pl.BlockSpec((1, dil, tm // dil, n), lambda b, i: (b, 0, i, 0))

    dils = [dil for _, dil in SWA_PATTERNS]
    return pl.pallas_call(
        _swa_out_kernel,
        grid=(batch, seq // tm),
        in_specs=[group_tile(dil, hd) for dil in dils] + [group_tile(dil, LANES) for dil in dils]
        + [pl.BlockSpec((hd, d), lambda b, i: (0, 0)),
           pl.BlockSpec((1, tm, d), lambda b, i: (b, i, 0))],
        out_specs=pl.BlockSpec((1, tm, d), lambda b, i: (b, i, 0)),
        out_shape=jax.ShapeDtypeStruct((batch, seq, d), jnp.float32),
        compiler_params=_cparams(("parallel", "parallel")),
        name="swa_merge_out",
    )(*outs, *lses, w, x3d)


def kernel(x, norm_mix, norm_mlp, gla_w_in, gla_w_gate_up, gla_b_gate, gla_g_out, gla_w_out,
           swa_w_qkv, swa_g_q, swa_g_k, swa_w_out, mlp_w_up, mlp_w_down):
    batch, seq, d = x.shape
    depth = norm_mix.shape[0]
    hk = GLA_HEADS * GLA_DK
    hv = GLA_HEADS * GLA_DV
    n_main = 2 * hk + 2 * hv
    bf = jnp.bfloat16
    assert seq % GLA_BLOCK == 0
    for window, dilation in SWA_PATTERNS:
        assert seq % window == 0

    x2d = x.reshape(batch * seq, d)
    for i in range(depth):
        j = i // 2
        if i % 2 == 0:
            w_in = gla_w_in[j]
            w_z = jnp.pad(w_in[:, n_main:], ((0, 0), (0, LANES - GLA_GATE_RANK))).astype(bf)
            qkvr, z = _gla_in_proj(x2d, norm_mix[i], w_in[:, :n_main].astype(bf), w_z,
                                   tm=1024, tn=1024)
            w_gate = jnp.pad(gla_w_gate_up[j], ((0, LANES - GLA_GATE_RANK), (0, 0))).astype(bf)
            o = _gla_core(qkvr.reshape(batch, seq, n_main), z.reshape(batch, seq, LANES),
                          w_gate, gla_b_gate[j].reshape(1, hk), gla_g_out[j].reshape(1, hv),
                          batch=batch, seq=seq)
            x2d = _matmul_residual(o.reshape(batch * seq, hv), gla_w_out[j].astype(bf), x2d,
                                   tm=1024)
        else:
            x3d = x2d.reshape(batch, seq, d)
            w_qkv = swa_w_qkv[j].astype(bf)
            outs, lses = [], []
            for gi, (window, dil) in enumerate(SWA_PATTERNS):
                assert window // dil == SWA_SPAN
                gains = jnp.stack([swa_g_q[j, gi] * SWA_HEAD_DIM ** -0.5, swa_g_k[j, gi]])
                qkv = _swa_proj(x3d, norm_mix[i], w_qkv, gains, group=gi, dil=dil, tm=1024)
                o, lse = _swa_group(qkv, dil=dil, rows=min(512, seq // dil))
                outs.append(o)
                lses.append(lse)
            x2d = _swa_out(outs, lses, swa_w_out[j].astype(bf), x3d).reshape(batch * seq, d)
        x2d = _mlp(x2d, norm_mlp[i], mlp_w_up[i].astype(bf), mlp_w_down[i].astype(bf),
                   tm=1024, tf=1024)
    return x2d.reshape(batch, seq, d)
```

```python
import functools

import jax
import jax.numpy as jnp
from jax import lax
from jax.experimental import pallas as pl
from jax.experimental.pallas import tpu as pltpu

EPS = 1e-6

GLA_HEADS = 4
GLA_DK = 128
GLA_DV = 256
GLA_GATE_RANK = 16
GLA_GATE_TAU = 16.0
GLA_CHUNK = 64
GLA_SUB = 16
GLA_BLOCK = 512
GLA_FAST_CHUNK = 256
GLA_FAST_MAX_DECAY = 40.0

SWA_PATTERNS = ((128, 1), (512, 4), (2048, 16))
SWA_HEADS = 8
SWA_HEAD_DIM = 128
SWA_SPAN = 128
PERM_ROWS = 256

LANES = 128
VMEM_LIMIT_BYTES = 56 * 1024 * 1024


def _cparams(semantics):
    return pltpu.CompilerParams(dimension_semantics=semantics,
                                vmem_limit_bytes=VMEM_LIMIT_BYTES)


def _rms_scale(x32):
    return lax.rsqrt(jnp.mean(x32 * x32, axis=-1, keepdims=True) + EPS)


def _perm_matrix(dil, to_natural):
    n = PERM_ROWS // dil
    row = lax.broadcasted_iota(jnp.int32, (PERM_ROWS, PERM_ROWS), 0)
    col = lax.broadcasted_iota(jnp.int32, (PERM_ROWS, PERM_ROWS), 1)
    tok, rm = (row, col) if to_natural else (col, row)
    rm_of_tok = (tok & (dil - 1)) * n + (tok >> (dil.bit_length() - 1))
    return jnp.where(rm == rm_of_tok, 1.0, 0.0).astype(jnp.bfloat16)


def _norm_rows(x32, g):
    return (x32 * _rms_scale(x32) * g).astype(jnp.bfloat16)


def _pipelined_tiles(norm_into, project_from):
    t = pl.program_id(0)

    @pl.when(t == 0)
    def _():
        norm_into(0)

    @pl.when((t > 0) & (t % 2 == 1))
    def _():
        norm_into(1)
        project_from(0)

    @pl.when((t > 0) & (t % 2 == 0))
    def _():
        norm_into(0)
        project_from(1)


def _swa_proj_kernel(x_ref, g_ref, w_ref, gqk_ref, o_ref, h_scr, *, dil):
    tm = x_ref.shape[0]
    n = PERM_ROWS // dil
    hd = SWA_HEADS * SWA_HEAD_DIM

    def norm_into(slot):
        h = _norm_rows(x_ref[...], g_ref[...])
        if dil == 1:
            h_scr[slot] = h
        else:
            perm = _perm_matrix(dil, to_natural=False)
            for sb in range(tm // PERM_ROWS):
                rows = slice(sb * PERM_ROWS, (sb + 1) * PERM_ROWS)
                h_scr[slot, rows, :] = jnp.dot(
                    perm, h[rows], preferred_element_type=jnp.float32).astype(jnp.bfloat16)

    def store(y_bf16, cols):
        if dil == 1:
            o_ref[0, 0, :, cols] = y_bf16
        else:
            for sb in range(tm // PERM_ROWS):
                for r in range(dil):
                    src = sb * PERM_ROWS + r * n
                    o_ref[0, r, sb * n:(sb + 1) * n, cols] = y_bf16[src:src + n]

    def project_from(slot):
        for j in range(3):
            y = jnp.dot(h_scr[slot], w_ref[:, j * hd:(j + 1) * hd],
                        preferred_element_type=jnp.float32)
            if j == 2:
                store(y.astype(jnp.bfloat16), slice(j * hd, (j + 1) * hd))
                continue
            gain = gqk_ref[j:j + 1, :]
            for h in range(SWA_HEADS):
                yh = y[:, h * SWA_HEAD_DIM:(h + 1) * SWA_HEAD_DIM]
                store((yh * _rms_scale(yh) * gain).astype(jnp.bfloat16),
                      slice(j * hd + h * SWA_HEAD_DIM, j * hd + (h + 1) * SWA_HEAD_DIM))

    _pipelined_tiles(norm_into, project_from)


def _swa_proj(x2d, g, w_qkv, gains_qk, *, batch, seq, group, dil, tm):
    d = x2d.shape[1]
    gcols = 3 * SWA_HEADS * SWA_HEAD_DIM
    tiles_per_seq = seq // tm
    n_tiles = batch * tiles_per_seq

    def out_map(t):
        tile = jnp.maximum(t - 1, 0)
        return (tile // tiles_per_seq, 0, tile % tiles_per_seq, 0)

    return pl.pallas_call(
        functools.partial(_swa_proj_kernel, dil=dil),
        grid=(n_tiles + 1,),
        in_specs=[
            pl.BlockSpec((tm, d), lambda t: (jnp.minimum(t, n_tiles - 1), 0)),
            pl.BlockSpec((1, d), lambda t: (0, 0)),
            pl.BlockSpec((d, gcols), lambda t: (0, group)),
            pl.BlockSpec((2, SWA_HEAD_DIM), lambda t: (0, 0)),
        ],
        out_specs=pl.BlockSpec((1, dil, tm // dil, gcols), out_map),
        out_shape=jax.ShapeDtypeStruct((batch, dil, seq // dil, gcols), jnp.bfloat16),
        scratch_shapes=[pltpu.VMEM((2, tm, d), jnp.bfloat16)],
        compiler_params=_cparams(("arbitrary",)),
        name=f"swa_proj_d{dil}",
    )(x2d, g.reshape(1, d), w_qkv, gains_qk)


def _gla_in_kernel(x_ref, g_ref, w_ref, wz_ref, o_ref, z_ref, h_scr):
    def norm_into(slot):
        h_scr[slot] = _norm_rows(x_ref[...], g_ref[...])

    def project_from(slot):
        h = h_scr[slot]
        z_ref[...] = jnp.dot(h, wz_ref[...], preferred_element_type=jnp.float32)
        tn = GLA_HEADS * GLA_DV
        for j in range(w_ref.shape[1] // tn):
            cols = slice(j * tn, (j + 1) * tn)
            o_ref[:, cols] = jnp.dot(h, w_ref[:, cols],
                                     preferred_element_type=jnp.float32).astype(o_ref.dtype)

    _pipelined_tiles(norm_into, project_from)


def _gla_in_proj(x2d, g, w_main, w_z, *, tm):
    m, d = x2d.shape
    n = w_main.shape[1]
    nz = w_z.shape[1]
    n_tiles = m // tm
    cur = lambda t: (jnp.maximum(t - 1, 0), 0)
    return pl.pallas_call(
        _gla_in_kernel,
        grid=(n_tiles + 1,),
        in_specs=[
            pl.BlockSpec((tm, d), lambda t: (jnp.minimum(t, n_tiles - 1), 0)),
            pl.BlockSpec((1, d), lambda t: (0, 0)),
            pl.BlockSpec((d, n), lambda t: (0, 0)),
            pl.BlockSpec((d, nz), lambda t: (0, 0)),
        ],
        out_specs=[pl.BlockSpec((tm, n), cur), pl.BlockSpec((tm, nz), cur)],
        out_shape=[
            jax.ShapeDtypeStruct((m, n), jnp.bfloat16),
            jax.ShapeDtypeStruct((m, nz), jnp.float32),
        ],
        scratch_shapes=[pltpu.VMEM((2, tm, d), jnp.bfloat16)],
        compiler_params=_cparams(("arbitrary",)),
        name="gla_in_proj",
    )(x2d, g.reshape(1, d), w_main, w_z)


def _gla_out_kernel(o_ref, r_ref, go_ref, w_ref, x_ref, out_ref):
    parts = []
    for h in range(GLA_HEADS):
        vs = slice(h * GLA_DV, (h + 1) * GLA_DV)
        o = o_ref[:, vs].astype(jnp.float32)
        r = r_ref[:, vs].astype(jnp.float32)
        y = o * _rms_scale(o) * go_ref[:, vs]
        parts.append((y * (r / (1.0 + jnp.exp(-r)))).astype(jnp.bfloat16))
    y = jnp.concatenate(parts, axis=-1)
    out_ref[...] = x_ref[...] + jnp.dot(y, w_ref[...], preferred_element_type=jnp.float32)


def _gla_out(o2d, qkvr2d, g_out, w, x2d, *, tm):
    m, d = x2d.shape
    hv = o2d.shape[1]
    r_block = qkvr2d.shape[1] // hv - 1
    return pl.pallas_call(
        _gla_out_kernel,
        grid=(m // tm,),
        in_specs=[
            pl.BlockSpec((tm, hv), lambda i: (i, 0)),
            pl.BlockSpec((tm, hv), lambda i: (i, r_block)),
            pl.BlockSpec((1, hv), lambda i: (0, 0)),
            pl.BlockSpec((hv, d), lambda i: (0, 0)),
            pl.BlockSpec((tm, d), lambda i: (i, 0)),
        ],
        out_specs=pl.BlockSpec((tm, d), lambda i: (i, 0)),
        out_shape=jax.ShapeDtypeStruct((m, d), jnp.float32),
        compiler_params=_cparams(("parallel",)),
        name="gla_out",
    )(o2d, qkvr2d, g_out, w, x2d)


def _mlp_kernel(x_ref, xn_ref, g_ref, wu_ref, wd_ref, out_ref, h_scr):
    i = pl.program_id(0)
    f = pl.program_id(1)
    last = pl.num_programs(1) - 1

    @pl.when((i == 0) & (f == 0))
    def _():
        h_scr[...] = _norm_rows(x_ref[...], g_ref[...])

    def hidden():
        u = jnp.dot(h_scr[...], wu_ref[...], preferred_element_type=jnp.float32)
        return jnp.square(jnp.maximum(u, 0.0)).astype(jnp.bfloat16)

    def down(a):
        return jnp.dot(a, wd_ref[...], preferred_element_type=jnp.float32)

    @pl.when(f == 0)
    def _():
        out_ref[...] = x_ref[...] + down(hidden())

    @pl.when((f > 0) & (f < last))
    def _():
        out_ref[...] += down(hidden())

    @pl.when(f == last)
    def _():
        a = hidden()
        h_scr[...] = _norm_rows(xn_ref[...], g_ref[...])
        out_ref[...] += down(a)


def _mlp(x2d, g, w_up, w_down, *, tm, tf):
    m, d = x2d.shape
    ff = w_up.shape[1]
    n_tiles = m // tm
    assert ff // tf >= 2
    return pl.pallas_call(
        _mlp_kernel,
        grid=(n_tiles, ff // tf),
        in_specs=[
            pl.BlockSpec((tm, d), lambda i, f: (i, 0)),
            pl.BlockSpec((tm, d), lambda i, f: (jnp.minimum(i + 1, n_tiles - 1), 0)),
            pl.BlockSpec((1, d), lambda i, f: (0, 0)),
            pl.BlockSpec((d, tf), lambda i, f: (0, f)),
            pl.BlockSpec((tf, d), lambda i, f: (f, 0)),
        ],
        out_specs=pl.BlockSpec((tm, d), lambda i, f: (i, 0)),
        out_shape=jax.ShapeDtypeStruct((m, d), jnp.float32),
        scratch_shapes=[pltpu.VMEM((tm, d), jnp.bfloat16)],
        compiler_params=_cparams(("arbitrary", "arbitrary")),
        name="sqrelu_mlp",
    )(x2d, x2d, g.reshape(1, d), w_up, w_down)


def _dot_nt(a, b):
    return lax.dot_general(a, b, (((1,), (1,)), ((), ())),
                           preferred_element_type=jnp.float32)


def _dot_tn(a, b):
    return lax.dot_general(a, b, (((0,), (0,)), ((), ())),
                           preferred_element_type=jnp.float32)


def _gla_kernel(q_ref, k_ref, v_ref, z_ref, wg_ref, bg_ref, o_ref,
                state_scr, b_scr, k_scr, brel_scr):
    C, SUB = GLA_CHUNK, GLA_SUB
    nsub = C // SUB
    nchunk = GLA_BLOCK // C

    @pl.when(pl.program_id(1) == 0)
    def _():
        state_scr[...] = jnp.zeros_like(state_scr)

    CF = GLA_FAST_CHUNK
    row = lax.broadcasted_iota(jnp.int32, (CF, CF), 0)
    col = lax.broadcasted_iota(jnp.int32, (CF, CF), 1)
    causal = row >= col
    tri = jnp.where(causal, 1.0, 0.0).astype(jnp.bfloat16)

    zb = z_ref[0].astype(jnp.bfloat16)
    gp = jnp.dot(zb, wg_ref[...], preferred_element_type=jnp.float32) + bg_ref[...]
    g = (jnp.minimum(gp, 0.0) - jnp.log(1.0 + jnp.exp(-jnp.abs(gp)))) * (1.0 / GLA_GATE_TAU)
    g_hi = g.astype(jnp.bfloat16)
    g_lo = (g - g_hi.astype(jnp.float32)).astype(jnp.bfloat16)
    decay_max = jnp.zeros((1, GLA_DK), jnp.float32)
    for h in range(GLA_HEADS):
        ks = slice(h * GLA_DK, (h + 1) * GLA_DK)
        for cf in range(GLA_BLOCK // CF):
            rows = slice(cf * CF, (cf + 1) * CF)
            g_hl = jnp.concatenate([g_hi[rows, ks], g_lo[rows, ks]], axis=-1)
            b_hl = jnp.dot(tri, g_hl, preferred_element_type=jnp.float32)
            b = b_hl[:, :GLA_DK] + b_hl[:, GLA_DK:]
            b_scr[h, rows, :] = b
            decay_max = jnp.maximum(decay_max, -b[CF - 1:CF, :])
    mild = jnp.max(decay_max) <= GLA_FAST_MAX_DECAY

    def chunk_update(ci_rows, h, st, a, q_in, k, v, b):
        vs = slice(h * GLA_DV, (h + 1) * GLA_DV)
        b_last = b[b.shape[0] - 1:, :]
        o = (jnp.dot(a, v, preferred_element_type=jnp.float32)
             + _dot_nt(q_in, st.astype(jnp.bfloat16)))
        o_ref[0, ci_rows, vs] = o.astype(o_ref.dtype)
        k_dec = (k * jnp.exp(b_last - b)).astype(jnp.bfloat16)
        return st * jnp.exp(b_last) + _dot_tn(v, k_dec)

    @pl.when(mild)
    def _():
        for cf in range(GLA_BLOCK // CF):
            rows = slice(cf * CF, (cf + 1) * CF)
            for h in range(GLA_HEADS):
                ks = slice(h * GLA_DK, (h + 1) * GLA_DK)
                vs = slice(h * GLA_DV, (h + 1) * GLA_DV)
                b = b_scr[h, rows, :]
                q = q_ref[0, rows, ks].astype(jnp.float32) * (GLA_DK ** -0.5)
                k = k_ref[0, rows, ks].astype(jnp.float32)
                v = v_ref[0, rows, vs]
                q_in = (q * jnp.exp(b)).astype(jnp.bfloat16)
                k_out = (k * jnp.exp(-b)).astype(jnp.bfloat16)
                a = jnp.where(causal, _dot_nt(q_in, k_out), 0.0).astype(jnp.bfloat16)
                state_scr[h] = chunk_update(rows, h, state_scr[h], a, q_in, k, v, b)

    @pl.when(jnp.logical_not(mild))
    def _():
        srow = lax.broadcasted_iota(jnp.int32, (SUB, C), 0)
        scol = lax.broadcasted_iota(jnp.int32, (SUB, C), 1)

        def chunk_body(ci, carry):
            rows = pl.ds(pl.multiple_of(ci * C, C), C)
            for h in range(GLA_HEADS):
                ks = slice(h * GLA_DK, (h + 1) * GLA_DK)
                vs = slice(h * GLA_DV, (h + 1) * GLA_DV)
                c0 = ci * C
                before = b_scr[h, pl.ds(jnp.maximum(c0 - 1, 0), 1), :]
                base = jnp.where(c0 % CF == 0, 0.0, before)
                b = b_scr[h, rows, :] - base
                q = q_ref[0, rows, ks].astype(jnp.float32) * (GLA_DK ** -0.5)
                k = k_ref[0, rows, ks].astype(jnp.float32)
                v = v_ref[0, rows, vs]
                k_scr[h] = k
                brel_scr[h] = b

                refs = [jnp.zeros((1, GLA_DK), jnp.float32)]
                for i in range(1, nsub):
                    refs.append(brel_scr[h, i * SUB - 1:i * SUB, :])
                ref_rows = jnp.concatenate(
                    [jnp.broadcast_to(r, (SUB, GLA_DK)) for r in refs], axis=0)
                qe = (q * jnp.exp(b - ref_rows)).astype(jnp.bfloat16)

                a_rows = []
                for i in range(nsub):
                    r0 = i * SUB
                    qi = q[r0:r0 + SUB]
                    bi = b[r0:r0 + SUB]
                    key = jnp.where(srow >= scol - r0, scol, -1)
                    blk = jnp.zeros((SUB, C), jnp.float32)
                    for jj in range(SUB):
                        j = r0 + jj
                        bj = brel_scr[h, j:j + 1, :]
                        kj = k_scr[h, j:j + 1, :]
                        e = jnp.exp(jnp.minimum(bi - bj, 0.0))
                        s = jnp.sum(qi * e * kj, axis=-1, keepdims=True)
                        blk = jnp.where(key == j, s, blk)
                    if i > 0:
                        ke = (k * jnp.exp(jnp.minimum(refs[i] - b, 0.0))).astype(jnp.bfloat16)
                        off = _dot_nt(qe[r0:r0 + SUB], ke)
                        blk = jnp.where(scol < r0, off, blk)
                    a_rows.append(blk)
                a = jnp.concatenate(a_rows, axis=0).astype(jnp.bfloat16)
                q_in = (q * jnp.exp(b)).astype(jnp.bfloat16)
                state_scr[h] = chunk_update(rows, h, state_scr[h], a, q_in, k, v, b)
            return carry

        lax.fori_loop(0, nchunk, chunk_body, 0)


def _gla_core(qkvr, z, w_gate, b_gate, *, batch, seq):
    hk = GLA_HEADS * GLA_DK
    hv = GLA_HEADS * GLA_DV
    tb = GLA_BLOCK
    tok = lambda col: (lambda b, t: (b, t, col))
    return pl.pallas_call(
        _gla_kernel,
        grid=(batch, seq // tb),
        in_specs=[
            pl.BlockSpec((1, tb, hk), tok(0)),
            pl.BlockSpec((1, tb, hk), tok(1)),
            pl.BlockSpec((1, tb, hv), tok(1)),
            pl.BlockSpec((1, tb, LANES), tok(0)),
            pl.BlockSpec((LANES, hk), lambda b, t: (0, 0)),
            pl.BlockSpec((1, hk), lambda b, t: (0, 0)),
        ],
        out_specs=pl.BlockSpec((1, tb, hv), tok(0)),
        out_shape=jax.ShapeDtypeStruct((batch, seq, hv), jnp.bfloat16),
        scratch_shapes=[
            pltpu.VMEM((GLA_HEADS, GLA_DV, GLA_DK), jnp.float32),
            pltpu.VMEM((GLA_HEADS, GLA_BLOCK, GLA_DK), jnp.float32),
            pltpu.VMEM((GLA_HEADS, GLA_CHUNK, GLA_DK), jnp.float32),
            pltpu.VMEM((GLA_HEADS, GLA_CHUNK, GLA_DK), jnp.float32),
        ],
        compiler_params=_cparams(("parallel", "arbitrary")),
        name="gla_core",
    )(qkvr, qkvr, qkvr, z, w_gate, b_gate)


def _swa_kernel(q_ref, k_ref, v_ref, o_ref, lse_ref, kprev_scr, vprev_scr, *, nsub):
    blk = SWA_SPAN
    first = pl.program_id(2) == 0

    @pl.when(first)
    def _():
        kprev_scr[...] = jnp.zeros_like(kprev_scr)
        vprev_scr[...] = jnp.zeros_like(vprev_scr)

    qi = lax.broadcasted_iota(jnp.int32, (blk, 2 * blk), 0)
    kc = lax.broadcasted_iota(jnp.int32, (blk, 2 * blk), 1)
    band = (kc >= qi) & (kc <= qi + blk)
    lo = jnp.where(first, blk, 0)
    band_first = (kc >= jnp.maximum(qi, lo)) & (kc <= qi + blk)
    lane = lax.broadcasted_iota(jnp.int32, (blk, LANES), 1)
    for s in range(nsub):
        rows = slice(s * blk, (s + 1) * blk)
        valid = band_first if s == 0 else band
        lse_tile = jnp.zeros((blk, LANES), jnp.float32)
        for h in range(SWA_HEADS):
            hs = slice(h * SWA_HEAD_DIM, (h + 1) * SWA_HEAD_DIM)
            q = q_ref[0, 0, rows, hs]
            if s == 0:
                k = jnp.concatenate([kprev_scr[:, hs], k_ref[0, 0, rows, hs]], axis=0)
                v = jnp.concatenate([vprev_scr[:, hs], v_ref[0, 0, rows, hs]], axis=0)
            else:
                k = k_ref[0, 0, (s - 1) * blk:(s + 1) * blk, hs]
                v = v_ref[0, 0, (s - 1) * blk:(s + 1) * blk, hs]
            sc = jnp.where(valid, _dot_nt(q, k), -jnp.inf)
            m = jnp.max(sc, axis=-1, keepdims=True)
            p = jnp.exp(sc - m)
            l = jnp.sum(p, axis=-1, keepdims=True)
            o = jnp.dot(p.astype(jnp.bfloat16), v, preferred_element_type=jnp.float32) / l
            o_ref[0, 0, rows, hs] = o.astype(o_ref.dtype)
            lse_tile = jnp.where(lane == h, m + jnp.log(l), lse_tile)
        lse_ref[0, 0, rows, :] = lse_tile
    kprev_scr[...] = k_ref[0, 0, (nsub - 1) * blk:nsub * blk, :]
    vprev_scr[...] = v_ref[0, 0, (nsub - 1) * blk:nsub * blk, :]


def _swa_group(qkv, *, dil, rows):
    batch, _, sub_len, _ = qkv.shape
    hd = SWA_HEADS * SWA_HEAD_DIM
    col = lambda c: (lambda b, r, j: (b, r, j, c))
    return pl.pallas_call(
        functools.partial(_swa_kernel, nsub=rows // SWA_SPAN),
        grid=(batch, dil, sub_len // rows),
        in_specs=[pl.BlockSpec((1, 1, rows, hd), col(0)),
                  pl.BlockSpec((1, 1, rows, hd), col(1)),
                  pl.BlockSpec((1, 1, rows, hd), col(2))],
        out_specs=[pl.BlockSpec((1, 1, rows, hd), col(0)),
                   pl.BlockSpec((1, 1, rows, LANES), col(0))],
        out_shape=[jax.ShapeDtypeStruct((batch, dil, sub_len, hd), jnp.bfloat16),
                   jax.ShapeDtypeStruct((batch, dil, sub_len, LANES), jnp.float32)],
        scratch_shapes=[pltpu.VMEM((SWA_SPAN, hd), jnp.bfloat16),
                        pltpu.VMEM((SWA_SPAN, hd), jnp.bfloat16)],
        compiler_params=_cparams(("parallel", "parallel", "arbitrary")),
        name=f"swa_attn_d{dil}",
    )(qkv, qkv, qkv)


def _swa_out_kernel(o1_ref, o2_ref, o3_ref, l1_ref, l2_ref, l3_ref, w_ref, x_ref, out_ref):
    def natural_bf16(ref, dil):
        rm = ref[0].reshape(PERM_ROWS, ref.shape[-1])
        if dil == 1:
            return rm.astype(jnp.float32)
        return jnp.dot(_perm_matrix(dil, to_natural=True), rm,
                       preferred_element_type=jnp.float32)

    def natural_f32(ref, dil):
        rm = ref[0].reshape(PERM_ROWS, ref.shape[-1])
        if dil == 1:
            return rm
        perm = _perm_matrix(dil, to_natural=True)
        hi = rm.astype(jnp.bfloat16)
        r1 = rm - hi.astype(jnp.float32)
        mid = r1.astype(jnp.bfloat16)
        lo = (r1 - mid.astype(jnp.float32)).astype(jnp.bfloat16)
        return (jnp.dot(perm, hi, preferred_element_type=jnp.float32)
                + jnp.dot(perm, mid, preferred_element_type=jnp.float32)
                + jnp.dot(perm, lo, preferred_element_type=jnp.float32))

    dils = [d for _, d in SWA_PATTERNS]
    l1, l2, l3 = (natural_f32(r, d) for r, d in zip((l1_ref, l2_ref, l3_ref), dils))
    o1, o2, o3 = (natural_bf16(r, d) for r, d in zip((o1_ref, o2_ref, o3_ref), dils))
    mx = jnp.maximum(jnp.maximum(l1, l2), l3)
    e1, e2, e3 = jnp.exp(l1 - mx), jnp.exp(l2 - mx), jnp.exp(l3 - mx)
    inv = 1.0 / (e1 + e2 + e3)
    w1, w2, w3 = e1 * inv, e2 * inv, e3 * inv
    parts = []
    for h in range(SWA_HEADS):
        hs = slice(h * SWA_HEAD_DIM, (h + 1) * SWA_HEAD_DIM)
        parts.append((w1[:, h:h + 1] * o1[:, hs] + w2[:, h:h + 1] * o2[:, hs]
                      + w3[:, h:h + 1] * o3[:, hs]).astype(jnp.bfloat16))
    o = jnp.concatenate(parts, axis=-1)
    out_ref[0] = x_ref[0] + jnp.dot(o, w_ref[...], preferred_element_type=jnp.float32)


def _swa_out(outs, lses, w, x3d):
    batch, seq, d = x3d.shape
    hd = w.shape[0]
    tm = PERM_ROWS

    def group_tile(dil, n):
        return pl.BlockSpec((1, dil, tm // dil, n), lambda b, i: (b, 0, i, 0))

    dils = [dil for _, dil in SWA_PATTERNS]
    return pl.pallas_call(
        _swa_out_kernel,
        grid=(batch, seq // tm),
        in_specs=[group_tile(dil, hd) for dil in dils] + [group_tile(dil, LANES) for dil in dils]
        + [pl.BlockSpec((hd, d), lambda b, i: (0, 0)),
           pl.BlockSpec((1, tm, d), lambda b, i: (b, i, 0))],
        out_specs=pl.BlockSpec((1, tm, d), lambda b, i: (b, i, 0)),
        out_shape=jax.ShapeDtypeStruct((batch, seq, d), jnp.float32),
        compiler_params=_cparams(("parallel", "parallel")),
        name="swa_merge_out",
    )(*outs, *lses, w, x3d)


def kernel(x, norm_mix, norm_mlp, gla_w_in, gla_w_gate_up, gla_b_gate, gla_g_out, gla_w_out,
           swa_w_qkv, swa_g_q, swa_g_k, swa_w_out, mlp_w_up, mlp_w_down):
    batch, seq, d = x.shape
    depth = norm_mix.shape[0]
    hk = GLA_HEADS * GLA_DK
    hv = GLA_HEADS * GLA_DV
    n_main = 2 * hk + 2 * hv
    bf = jnp.bfloat16
    assert seq % GLA_BLOCK == 0
    for window, dilation in SWA_PATTERNS:
        assert seq % window == 0

    x2d = x.reshape(batch * seq, d)
    for i in range(depth):
        j = i // 2
        if i % 2 == 0:
            w_in = gla_w_in[j]
            w_z = jnp.pad(w_in[:, n_main:], ((0, 0), (0, LANES - GLA_GATE_RANK))).astype(bf)
            qkvr, z = _gla_in_proj(x2d, norm_mix[i], w_in[:, :n_main].astype(bf), w_z, tm=1024)
            w_gate = jnp.pad(gla_w_gate_up[j], ((0, LANES - GLA_GATE_RANK), (0, 0))).astype(bf)
            o = _gla_core(qkvr.reshape(batch, seq, n_main), z.reshape(batch, seq, LANES),
                          w_gate, gla_b_gate[j].reshape(1, hk), batch=batch, seq=seq)
            x2d = _gla_out(o.reshape(batch * seq, hv), qkvr, gla_g_out[j].reshape(1, hv),
                           gla_w_out[j].astype(bf), x2d, tm=1024)
        else:
            w_qkv = swa_w_qkv[j].astype(bf)
            outs, lses = [], []
            for gi, (window, dil) in enumerate(SWA_PATTERNS):
                assert window // dil == SWA_SPAN
                gains = jnp.stack([swa_g_q[j, gi] * SWA_HEAD_DIM ** -0.5, swa_g_k[j, gi]])
                qkv = _swa_proj(x2d, norm_mix[i], w_qkv, gains, batch=batch, seq=seq, group=gi,
                                dil=dil, tm=1024)
                o, lse = _swa_group(qkv, dil=dil, rows=min(512, seq // dil))
                outs.append(o)
                lses.append(lse)
            x2d = _swa_out(outs, lses, swa_w_out[j].astype(bf),
                           x2d.reshape(batch, seq, d)).reshape(batch * seq, d)
        x2d = _mlp(x2d, norm_mlp[i], mlp_w_up[i].astype(bf), mlp_w_down[i].astype(bf),
                   tm=1024, tf=1024)
    return x2d.reshape(batch, seq, d)
```

```python
import functools

import jax
import jax.numpy as jnp
from jax import lax
from jax.experimental import pallas as pl
from jax.experimental.pallas import tpu as pltpu

EPS = 1e-6

GLA_HEADS = 4
GLA_DK = 128
GLA_DV = 256
GLA_GATE_RANK = 16
GLA_GATE_TAU = 16.0
GLA_CHUNK = 64
GLA_SUB = 16
GLA_BLOCK = 512
GLA_FAST_CHUNK = 256
GLA_FAST_MAX_DECAY = 40.0

SWA_PATTERNS = ((128, 1), (512, 4), (2048, 16))
SWA_HEADS = 8
SWA_HEAD_DIM = 128
SWA_SPAN = 128
PERM_ROWS = 256

LANES = 128
VMEM_LIMIT_BYTES = 56 * 1024 * 1024


def _cparams(semantics):
    return pltpu.CompilerParams(dimension_semantics=semantics,
                                vmem_limit_bytes=VMEM_LIMIT_BYTES)


def _rms_scale(x32):
    return lax.rsqrt(jnp.mean(x32 * x32, axis=-1, keepdims=True) + EPS)


def _perm_matrix(dil, to_natural):
    n = PERM_ROWS // dil
    row = lax.broadcasted_iota(jnp.int32, (PERM_ROWS, PERM_ROWS), 0)
    col = lax.broadcasted_iota(jnp.int32, (PERM_ROWS, PERM_ROWS), 1)
    tok, rm = (row, col) if to_natural else (col, row)
    rm_of_tok = (tok & (dil - 1)) * n + (tok >> (dil.bit_length() - 1))
    return jnp.where(rm == rm_of_tok, 1.0, 0.0).astype(jnp.bfloat16)


def _norm_rows(x32, g):
    return (x32 * _rms_scale(x32) * g).astype(jnp.bfloat16)


def _pipelined_tiles(norm_into, project_from):
    t = pl.program_id(0)

    @pl.when(t == 0)
    def _():
        norm_into(0)

    @pl.when((t > 0) & (t % 2 == 1))
    def _():
        norm_into(1)
        project_from(0)

    @pl.when((t > 0) & (t % 2 == 0))
    def _():
        norm_into(0)
        project_from(1)


def _swa_proj_kernel(x_ref, g_ref, w_ref, gqk_ref, o_ref, h_scr, *, dil):
    tm = x_ref.shape[0]
    n = PERM_ROWS // dil
    hd = SWA_HEADS * SWA_HEAD_DIM

    def norm_into(slot):
        h = _norm_rows(x_ref[...], g_ref[...])
        if dil == 1:
            h_scr[slot] = h
        else:
            perm = _perm_matrix(dil, to_natural=False)
            for sb in range(tm // PERM_ROWS):
                rows = slice(sb * PERM_ROWS, (sb + 1) * PERM_ROWS)
                h_scr[slot, rows, :] = jnp.dot(
                    perm, h[rows], preferred_element_type=jnp.float32).astype(jnp.bfloat16)

    def store(y_bf16, cols):
        if dil == 1:
            o_ref[0, 0, :, cols] = y_bf16
        else:
            for sb in range(tm // PERM_ROWS):
                for r in range(dil):
                    src = sb * PERM_ROWS + r * n
                    o_ref[0, r, sb * n:(sb + 1) * n, cols] = y_bf16[src:src + n]

    def project_from(slot):
        for j in range(3):
            y = jnp.dot(h_scr[slot], w_ref[:, j * hd:(j + 1) * hd],
                        preferred_element_type=jnp.float32)
            if j == 2:
                store(y.astype(jnp.bfloat16), slice(j * hd, (j + 1) * hd))
                continue
            gain = gqk_ref[j:j + 1, :]
            for h in range(SWA_HEADS):
                yh = y[:, h * SWA_HEAD_DIM:(h + 1) * SWA_HEAD_DIM]
                store((yh * _rms_scale(yh) * gain).astype(jnp.bfloat16),
                      slice(j * hd + h * SWA_HEAD_DIM, j * hd + (h + 1) * SWA_HEAD_DIM))

    _pipelined_tiles(norm_into, project_from)


def _swa_proj(x2d, g, w_qkv, gains_qk, *, layer, batch, seq, group, dil, tm):
    d = x2d.shape[1]
    gcols = 3 * SWA_HEADS * SWA_HEAD_DIM
    tiles_per_seq = seq // tm
    n_tiles = batch * tiles_per_seq

    def out_map(t):
        tile = jnp.maximum(t - 1, 0)
        return (tile // tiles_per_seq, 0, tile % tiles_per_seq, 0)

    return pl.pallas_call(
        functools.partial(_swa_proj_kernel, dil=dil),
        grid=(n_tiles + 1,),
        in_specs=[
            pl.BlockSpec((tm, d), lambda t: (jnp.minimum(t, n_tiles - 1), 0)),
            pl.BlockSpec((1, d), lambda t: (0, 0)),
            pl.BlockSpec((None, d, gcols), lambda t: (layer, 0, group)),
            pl.BlockSpec((2, SWA_HEAD_DIM), lambda t: (0, 0)),
        ],
        out_specs=pl.BlockSpec((1, dil, tm // dil, gcols), out_map),
        out_shape=jax.ShapeDtypeStruct((batch, dil, seq // dil, gcols), jnp.bfloat16),
        scratch_shapes=[pltpu.VMEM((2, tm, d), jnp.bfloat16)],
        compiler_params=_cparams(("arbitrary",)),
        name=f"swa_proj_d{dil}",
    )(x2d, g.reshape(1, d), w_qkv, gains_qk)


def _gla_in_kernel(x_ref, g_ref, w_ref, wz_ref, o_ref, z_ref, h_scr):
    def norm_into(slot):
        h_scr[slot] = _norm_rows(x_ref[...], g_ref[...])

    def project_from(slot):
        h = h_scr[slot]
        z_ref[...] = jnp.dot(h, wz_ref[...], preferred_element_type=jnp.float32)
        tn = GLA_HEADS * GLA_DV
        for j in range(w_ref.shape[1] // tn):
            cols = slice(j * tn, (j + 1) * tn)
            o_ref[:, cols] = jnp.dot(h, w_ref[:, cols],
                                     preferred_element_type=jnp.float32).astype(o_ref.dtype)

    _pipelined_tiles(norm_into, project_from)


def _gla_in_proj(x2d, g, w_main, w_z, *, tm):
    m, d = x2d.shape
    n = w_main.shape[1]
    nz = w_z.shape[1]
    n_tiles = m // tm
    cur = lambda t: (jnp.maximum(t - 1, 0), 0)
    return pl.pallas_call(
        _gla_in_kernel,
        grid=(n_tiles + 1,),
        in_specs=[
            pl.BlockSpec((tm, d), lambda t: (jnp.minimum(t, n_tiles - 1), 0)),
            pl.BlockSpec((1, d), lambda t: (0, 0)),
            pl.BlockSpec((d, n), lambda t: (0, 0)),
            pl.BlockSpec((d, nz), lambda t: (0, 0)),
        ],
        out_specs=[pl.BlockSpec((tm, n), cur), pl.BlockSpec((tm, nz), cur)],
        out_shape=[
            jax.ShapeDtypeStruct((m, n), jnp.bfloat16),
            jax.ShapeDtypeStruct((m, nz), jnp.float32),
        ],
        scratch_shapes=[pltpu.VMEM((2, tm, d), jnp.bfloat16)],
        compiler_params=_cparams(("arbitrary",)),
        name="gla_in_proj",
    )(x2d, g.reshape(1, d), w_main, w_z)


def _gla_out_kernel(o_ref, r_ref, go_ref, w_ref, x_ref, out_ref):
    parts = []
    for h in range(GLA_HEADS):
        vs = slice(h * GLA_DV, (h + 1) * GLA_DV)
        o = o_ref[:, vs].astype(jnp.float32)
        r = r_ref[:, vs].astype(jnp.float32)
        y = o * _rms_scale(o) * go_ref[:, vs]
        parts.append((y * (r / (1.0 + jnp.exp(-r)))).astype(jnp.bfloat16))
    y = jnp.concatenate(parts, axis=-1)
    out_ref[...] = x_ref[...] + jnp.dot(y, w_ref[...], preferred_element_type=jnp.float32)


def _gla_out(o2d, qkvr2d, g_out, w, x2d, *, layer, tm):
    m, d = x2d.shape
    hv = o2d.shape[1]
    r_block = qkvr2d.shape[1] // hv - 1
    return pl.pallas_call(
        _gla_out_kernel,
        grid=(m // tm,),
        in_specs=[
            pl.BlockSpec((tm, hv), lambda i: (i, 0)),
            pl.BlockSpec((tm, hv), lambda i: (i, r_block)),
            pl.BlockSpec((1, hv), lambda i: (0, 0)),
            pl.BlockSpec((None, hv, d), lambda i: (layer, 0, 0)),
            pl.BlockSpec((tm, d), lambda i: (i, 0)),
        ],
        out_specs=pl.BlockSpec((tm, d), lambda i: (i, 0)),
        out_shape=jax.ShapeDtypeStruct((m, d), jnp.float32),
        compiler_params=_cparams(("parallel",)),
        name="gla_out",
    )(o2d, qkvr2d, g_out, w, x2d)


def _mlp_kernel(x_ref, g_ref, wu_ref, wd_ref, out_ref, h_scr):
    i = pl.program_id(0)
    f = pl.program_id(1)
    last = pl.num_programs(1) - 1

    @pl.when((i == 0) & (f == 0))
    def _():
        h_scr[...] = _norm_rows(x_ref[...], g_ref[...])

    def hidden():
        u = jnp.dot(h_scr[...], wu_ref[...], preferred_element_type=jnp.float32)
        return jnp.square(jnp.maximum(u, 0.0)).astype(jnp.bfloat16)

    def down(a):
        return jnp.dot(a, wd_ref[...], preferred_element_type=jnp.float32)

    @pl.when(f == 0)
    def _():
        out_ref[...] = x_ref[...] + down(hidden())

    @pl.when((f > 0) & (f < last))
    def _():
        out_ref[...] += down(hidden())

    @pl.when(f == last)
    def _():
        a = hidden()
        h_scr[...] = _norm_rows(x_ref[...], g_ref[...])
        out_ref[...] += down(a)


def _mlp(x2d, g, w_up, w_down, *, layer, tm, tf):
    m, d = x2d.shape
    ff = w_up.shape[2]
    n_tiles = m // tm
    nf = ff // tf
    assert nf >= 2

    def x_map(i, f):
        return (jnp.minimum(i + f // (nf - 1), n_tiles - 1), 0)

    return pl.pallas_call(
        _mlp_kernel,
        grid=(n_tiles, nf),
        in_specs=[
            pl.BlockSpec((tm, d), x_map),
            pl.BlockSpec((1, d), lambda i, f: (0, 0)),
            pl.BlockSpec((None, d, tf), lambda i, f: (layer, 0, f)),
            pl.BlockSpec((None, tf, d), lambda i, f: (layer, f, 0)),
        ],
        out_specs=pl.BlockSpec((tm, d), lambda i, f: (i, 0)),
        out_shape=jax.ShapeDtypeStruct((m, d), jnp.float32),
        scratch_shapes=[pltpu.VMEM((tm, d), jnp.bfloat16)],
        compiler_params=_cparams(("arbitrary", "arbitrary")),
        name="sqrelu_mlp",
    )(x2d, g.reshape(1, d), w_up, w_down)


def _dot_nt(a, b):
    return lax.dot_general(a, b, (((1,), (1,)), ((), ())),
                           preferred_element_type=jnp.float32)


def _dot_tn(a, b):
    return lax.dot_general(a, b, (((0,), (0,)), ((), ())),
                           preferred_element_type=jnp.float32)


def _gla_kernel(q_ref, k_ref, v_ref, z_ref, wg_ref, bg_ref, o_ref,
                state_scr, b_scr, k_scr, brel_scr):
    C, SUB = GLA_CHUNK, GLA_SUB
    nsub = C // SUB
    nchunk = GLA_BLOCK // C

    @pl.when(pl.program_id(1) == 0)
    def _():
        state_scr[...] = jnp.zeros_like(state_scr)

    CF = GLA_FAST_CHUNK
    row = lax.broadcasted_iota(jnp.int32, (CF, CF), 0)
    col = lax.broadcasted_iota(jnp.int32, (CF, CF), 1)
    causal = row >= col
    tri = jnp.where(causal, 1.0, 0.0).astype(jnp.bfloat16)

    zb = z_ref[0].astype(jnp.bfloat16)
    gp = jnp.dot(zb, wg_ref[...], preferred_element_type=jnp.float32) + bg_ref[...]
    g = (jnp.minimum(gp, 0.0) - jnp.log(1.0 + jnp.exp(-jnp.abs(gp)))) * (1.0 / GLA_GATE_TAU)
    g_hi = g.astype(jnp.bfloat16)
    g_lo = (g - g_hi.astype(jnp.float32)).astype(jnp.bfloat16)
    decay_max = jnp.zeros((1, GLA_DK), jnp.float32)
    for h in range(GLA_HEADS):
        ks = slice(h * GLA_DK, (h + 1) * GLA_DK)
        for cf in range(GLA_BLOCK // CF):
            rows = slice(cf * CF, (cf + 1) * CF)
            g_hl = jnp.concatenate([g_hi[rows, ks], g_lo[rows, ks]], axis=-1)
            b_hl = jnp.dot(tri, g_hl, preferred_element_type=jnp.float32)
            b = b_hl[:, :GLA_DK] + b_hl[:, GLA_DK:]
            b_scr[h, rows, :] = b
            decay_max = jnp.maximum(decay_max, -b[CF - 1:CF, :])
    mild = jnp.max(decay_max) <= GLA_FAST_MAX_DECAY

    def chunk_update(ci_rows, h, st, a, q_in, k, v, b):
        vs = slice(h * GLA_DV, (h + 1) * GLA_DV)
        b_last = b[b.shape[0] - 1:, :]
        o = (jnp.dot(a, v, preferred_element_type=jnp.float32)
             + _dot_nt(q_in, st.astype(jnp.bfloat16)))
        o_ref[0, ci_rows, vs] = o.astype(o_ref.dtype)
        k_dec = (k * jnp.exp(b_last - b)).astype(jnp.bfloat16)
        return st * jnp.exp(b_last) + _dot_tn(v, k_dec)

    @pl.when(mild)
    def _():
        for cf in range(GLA_BLOCK // CF):
            rows = slice(cf * CF, (cf + 1) * CF)
            for h in range(GLA_HEADS):
                ks = slice(h * GLA_DK, (h + 1) * GLA_DK)
                vs = slice(h * GLA_DV, (h + 1) * GLA_DV)
                b = b_scr[h, rows, :]
                q = q_ref[0, rows, ks].astype(jnp.float32) * (GLA_DK ** -0.5)
                k = k_ref[0, rows, ks].astype(jnp.float32)
                v = v_ref[0, rows, vs]
                q_in = (q * jnp.exp(b)).astype(jnp.bfloat16)
                k_out = (k * jnp.exp(-b)).astype(jnp.bfloat16)
                a = jnp.where(causal, _dot_nt(q_in, k_out), 0.0).astype(jnp.bfloat16)
                state_scr[h] = chunk_update(rows, h, state_scr[h], a, q_in, k, v, b)

    @pl.when(jnp.logical_not(mild))
    def _():
        srow = lax.broadcasted_iota(jnp.int32, (SUB, C), 0)
        scol = lax.broadcasted_iota(jnp.int32, (SUB, C), 1)

        def chunk_body(ci, carry):
            rows = pl.ds(pl.multiple_of(ci * C, C), C)
            for h in range(GLA_HEADS):
                ks = slice(h * GLA_DK, (h + 1) * GLA_DK)
                vs = slice(h * GLA_DV, (h + 1) * GLA_DV)
                c0 = ci * C
                before = b_scr[h, pl.ds(jnp.maximum(c0 - 1, 0), 1), :]
                base = jnp.where(c0 % CF == 0, 0.0, before)
                b = b_scr[h, rows, :] - base
                q = q_ref[0, rows, ks].astype(jnp.float32) * (GLA_DK ** -0.5)
                k = k_ref[0, rows, ks].astype(jnp.float32)
                v = v_ref[0, rows, vs]
                k_scr[h] = k
                brel_scr[h] = b

                refs = [jnp.zeros((1, GLA_DK), jnp.float32)]
                for i in range(1, nsub):
                    refs.append(brel_scr[h, i * SUB - 1:i * SUB, :])
                ref_rows = jnp.concatenate(
                    [jnp.broadcast_to(r, (SUB, GLA_DK)) for r in refs], axis=0)
                qe = (q * jnp.exp(b - ref_rows)).astype(jnp.bfloat16)

                a_rows = []
                for i in range(nsub):
                    r0 = i * SUB
                    qi = q[r0:r0 + SUB]
                    bi = b[r0:r0 + SUB]
                    key = jnp.where(srow >= scol - r0, scol, -1)
                    blk = jnp.zeros((SUB, C), jnp.float32)
                    for jj in range(SUB):
                        j = r0 + jj
                        bj = brel_scr[h, j:j + 1, :]
                        kj = k_scr[h, j:j + 1, :]
                        e = jnp.exp(jnp.minimum(bi - bj, 0.0))
                        s = jnp.sum(qi * e * kj, axis=-1, keepdims=True)
                        blk = jnp.where(key == j, s, blk)
                    if i > 0:
                        ke = (k * jnp.exp(jnp.minimum(refs[i] - b, 0.0))).astype(jnp.bfloat16)
                        off = _dot_nt(qe[r0:r0 + SUB], ke)
                        blk = jnp.where(scol < r0, off, blk)
                    a_rows.append(blk)
                a = jnp.concatenate(a_rows, axis=0).astype(jnp.bfloat16)
                q_in = (q * jnp.exp(b)).astype(jnp.bfloat16)
                state_scr[h] = chunk_update(rows, h, state_scr[h], a, q_in, k, v, b)
            return carry

        lax.fori_loop(0, nchunk, chunk_body, 0)


def _gla_core(qkvr, z, w_gate, b_gate, *, batch, seq):
    hk = GLA_HEADS * GLA_DK
    hv = GLA_HEADS * GLA_DV
    tb = GLA_BLOCK
    tok = lambda col: (lambda b, t: (b, t, col))
    return pl.pallas_call(
        _gla_kernel,
        grid=(batch, seq // tb),
        in_specs=[
            pl.BlockSpec((1, tb, hk), tok(0)),
            pl.BlockSpec((1, tb, hk), tok(1)),
            pl.BlockSpec((1, tb, hv), tok(1)),
            pl.BlockSpec((1, tb, LANES), tok(0)),
            pl.BlockSpec((LANES, hk), lambda b, t: (0, 0)),
            pl.BlockSpec((1, hk), lambda b, t: (0, 0)),
        ],
        out_specs=pl.BlockSpec((1, tb, hv), tok(0)),
        out_shape=jax.ShapeDtypeStruct((batch, seq, hv), jnp.bfloat16),
        scratch_shapes=[
            pltpu.VMEM((GLA_HEADS, GLA_DV, GLA_DK), jnp.float32),
            pltpu.VMEM((GLA_HEADS, GLA_BLOCK, GLA_DK), jnp.float32),
            pltpu.VMEM((GLA_HEADS, GLA_CHUNK, GLA_DK), jnp.float32),
            pltpu.VMEM((GLA_HEADS, GLA_CHUNK, GLA_DK), jnp.float32),
        ],
        compiler_params=_cparams(("parallel", "arbitrary")),
        name="gla_core",
    )(qkvr, qkvr, qkvr, z, w_gate, b_gate)


def _swa_kernel(q_ref, k_ref, v_ref, o_ref, lse_ref, kprev_scr, vprev_scr, *, nsub):
    blk = SWA_SPAN
    first = pl.program_id(2) == 0

    @pl.when(first)
    def _():
        kprev_scr[...] = jnp.zeros_like(kprev_scr)
        vprev_scr[...] = jnp.zeros_like(vprev_scr)

    qi = lax.broadcasted_iota(jnp.int32, (blk, 2 * blk), 0)
    kc = lax.broadcasted_iota(jnp.int32, (blk, 2 * blk), 1)
    band = (kc >= qi) & (kc <= qi + blk)
    lo = jnp.where(first, blk, 0)
    band_first = (kc >= jnp.maximum(qi, lo)) & (kc <= qi + blk)
    lane = lax.broadcasted_iota(jnp.int32, (blk, LANES), 1)
    for s in range(nsub):
        rows = slice(s * blk, (s + 1) * blk)
        valid = band_first if s == 0 else band
        lse_tile = jnp.zeros((blk, LANES), jnp.float32)
        for h in range(SWA_HEADS):
            hs = slice(h * SWA_HEAD_DIM, (h + 1) * SWA_HEAD_DIM)
            q = q_ref[0, 0, rows, hs]
            if s == 0:
                k = jnp.concatenate([kprev_scr[:, hs], k_ref[0, 0, rows, hs]], axis=0)
                v = jnp.concatenate([vprev_scr[:, hs], v_ref[0, 0, rows, hs]], axis=0)
            else:
                k = k_ref[0, 0, (s - 1) * blk:(s + 1) * blk, hs]
                v = v_ref[0, 0, (s - 1) * blk:(s + 1) * blk, hs]
            sc = jnp.where(valid, _dot_nt(q, k), -jnp.inf)
            m = jnp.max(sc, axis=-1, keepdims=True)
            p = jnp.exp(sc - m)
            l = jnp.sum(p, axis=-1, keepdims=True)
            o = jnp.dot(p.astype(jnp.bfloat16), v, preferred_element_type=jnp.float32) / l
            o_ref[0, 0, rows, hs] = o.astype(o_ref.dtype)
            lse_tile = jnp.where(lane == h, m + jnp.log(l), lse_tile)
        lse_ref[0, 0, rows, :] = lse_tile
    kprev_scr[...] = k_ref[0, 0, (nsub - 1) * blk:nsub * blk, :]
    vprev_scr[...] = v_ref[0, 0, (nsub - 1) * blk:nsub * blk, :]


def _swa_group(qkv, *, dil, rows):
    batch, _, sub_len, _ = qkv.shape
    hd = SWA_HEADS * SWA_HEAD_DIM
    col = lambda c: (lambda b, r, j: (b, r, j, c))
    return pl.pallas_call(
        functools.partial(_swa_kernel, nsub=rows // SWA_SPAN),
        grid=(batch, dil, sub_len // rows),
        in_specs=[pl.BlockSpec((1, 1, rows, hd), col(0)),
                  pl.BlockSpec((1, 1, rows, hd), col(1)),
                  pl.BlockSpec((1, 1, rows, hd), col(2))],
        out_specs=[pl.BlockSpec((1, 1, rows, hd), col(0)),
                   pl.BlockSpec((1, 1, rows, LANES), col(0))],
        out_shape=[jax.ShapeDtypeStruct((batch, dil, sub_len, hd), jnp.bfloat16),
                   jax.ShapeDtypeStruct((batch, dil, sub_len, LANES), jnp.float32)],
        scratch_shapes=[pltpu.VMEM((SWA_SPAN, hd), jnp.bfloat16),
                        pltpu.VMEM((SWA_SPAN, hd), jnp.bfloat16)],
        compiler_params=_cparams(("parallel", "parallel", "arbitrary")),
        name=f"swa_attn_d{dil}",
    )(qkv, qkv, qkv)


def _swa_out_kernel(o1_ref, o2_ref, o3_ref, l1_ref, l2_ref, l3_ref, w_ref, x_ref, out_ref):
    tm = x_ref.shape[1]

    def rows_of(ref, dil, sb):
        n = PERM_ROWS // dil
        if dil == 1:
            return ref[0, 0, sb * PERM_ROWS:(sb + 1) * PERM_ROWS, :]
        return jnp.concatenate([ref[0, r, sb * n:(sb + 1) * n, :] for r in range(dil)], axis=0)

    def natural_bf16(ref, dil, sb):
        rm = rows_of(ref, dil, sb)
        if dil == 1:
            return rm.astype(jnp.float32)
        return jnp.dot(_perm_matrix(dil, to_natural=True), rm,
                       preferred_element_type=jnp.float32)

    def natural_f32(ref, dil, sb):
        rm = rows_of(ref, dil, sb)
        if dil == 1:
            return rm
        perm = _perm_matrix(dil, to_natural=True)
        hi = rm.astype(jnp.bfloat16)
        lo = (rm - hi.astype(jnp.float32)).astype(jnp.bfloat16)
        return (jnp.dot(perm, hi, preferred_element_type=jnp.float32)
                + jnp.dot(perm, lo, preferred_element_type=jnp.float32))

    dils = [d for _, d in SWA_PATTERNS]
    merged = []
    for sb in range(tm // PERM_ROWS):
        l1, l2, l3 = (natural_f32(r, d, sb) for r, d in zip((l1_ref, l2_ref, l3_ref), dils))
        o1, o2, o3 = (natural_bf16(r, d, sb) for r, d in zip((o1_ref, o2_ref, o3_ref), dils))
        mx = jnp.maximum(jnp.maximum(l1, l2), l3)
        e1, e2, e3 = jnp.exp(l1 - mx), jnp.exp(l2 - mx), jnp.exp(l3 - mx)
        inv = 1.0 / (e1 + e2 + e3)
        w2, w3 = e2 * inv, e3 * inv
        parts = []
        for h in range(SWA_HEADS):
            hs = slice(h * SWA_HEAD_DIM, (h + 1) * SWA_HEAD_DIM)
            base = o1[:, hs]
            parts.append((base + w2[:, h:h + 1] * (o2[:, hs] - base)
                          + w3[:, h:h + 1] * (o3[:, hs] - base)).astype(jnp.bfloat16))
        merged.append(jnp.concatenate(parts, axis=-1))
    o = jnp.concatenate(merged, axis=0)
    out_ref[0] = x_ref[0] + jnp.dot(o, w_ref[...], preferred_element_type=jnp.float32)


def _swa_out(outs, lses, w, x3d, *, layer, tm):
    batch, seq, d = x3d.shape
    hd = w.shape[1]
    assert tm % PERM_ROWS == 0

    def group_tile(dil, n):
        return pl.BlockSpec((1, dil, tm // dil, n), lambda b, i: (b, 0, i, 0))

    dils = [dil for _, dil in SWA_PATTERNS]
    return pl.pallas_call(
        _swa_out_kernel,
        grid=(batch, seq // tm),
        in_specs=[group_tile(dil, hd) for dil in dils] + [group_tile(dil, LANES) for dil in dils]
        + [pl.BlockSpec((None, hd, d), lambda b, i: (layer, 0, 0)),
           pl.BlockSpec((1, tm, d), lambda b, i: (b, i, 0))],
        out_specs=pl.BlockSpec((1, tm, d), lambda b, i: (b, i, 0)),
        out_shape=jax.ShapeDtypeStruct((batch, seq, d), jnp.float32),
        compiler_params=_cparams(("parallel", "parallel")),
        name="swa_merge_out",
    )(*outs, *lses, w, x3d)


def kernel(x, norm_mix, norm_mlp, gla_w_in, gla_w_gate_up, gla_b_gate, gla_g_out, gla_w_out,
           swa_w_qkv, swa_g_q, swa_g_k, swa_w_out, mlp_w_up, mlp_w_down):
    batch, seq, d = x.shape
    depth = norm_mix.shape[0]
    hk = GLA_HEADS * GLA_DK
    hv = GLA_HEADS * GLA_DV
    n_main = 2 * hk + 2 * hv
    bf = jnp.bfloat16
    assert seq % GLA_BLOCK == 0
    for window, dilation in SWA_PATTERNS:
        assert seq % window == 0

    gla_w_out_bf, swa_w_qkv_bf, swa_w_out_bf = (w.astype(bf) for w in
                                                (gla_w_out, swa_w_qkv, swa_w_out))
    mlp_w_up_bf, mlp_w_down_bf = mlp_w_up.astype(bf), mlp_w_down.astype(bf)

    x2d = x.reshape(batch * seq, d)
    for i in range(depth):
        j = i // 2
        if i % 2 == 0:
            w_in = gla_w_in[j]
            w_z = jnp.pad(w_in[:, n_main:], ((0, 0), (0, LANES - GLA_GATE_RANK))).astype(bf)
            qkvr, z = _gla_in_proj(x2d, norm_mix[i], w_in[:, :n_main].astype(bf), w_z, tm=1024)
            w_gate = jnp.pad(gla_w_gate_up[j], ((0, LANES - GLA_GATE_RANK), (0, 0))).astype(bf)
            o = _gla_core(qkvr.reshape(batch, seq, n_main), z.reshape(batch, seq, LANES),
                          w_gate, gla_b_gate[j].reshape(1, hk), batch=batch, seq=seq)
            x2d = _gla_out(o.reshape(batch * seq, hv), qkvr, gla_g_out[j].reshape(1, hv),
                           gla_w_out_bf, x2d, layer=j, tm=1024)
        else:
            outs, lses = [], []
            for gi, (window, dil) in enumerate(SWA_PATTERNS):
                assert window // dil == SWA_SPAN
                gains = jnp.stack([swa_g_q[j, gi] * SWA_HEAD_DIM ** -0.5, swa_g_k[j, gi]])
                qkv = _swa_proj(x2d, norm_mix[i], swa_w_qkv_bf, gains, layer=j, batch=batch,
                                seq=seq, group=gi, dil=dil, tm=1024)
                o, lse = _swa_group(qkv, dil=dil, rows=min(512, seq // dil))
                outs.append(o)
                lses.append(lse)
            x2d = _swa_out(outs, lses, swa_w_out_bf, x2d.reshape(batch, seq, d),
                           layer=j, tm=512).reshape(batch * seq, d)
        x2d = _mlp(x2d, norm_mlp[i], mlp_w_up_bf, mlp_w_down_bf, layer=i, tm=1024, tf=2048)
    return x2d.reshape(batch, seq, d)
```

```python
import functools

import jax
import jax.numpy as jnp
from jax import lax
from jax.experimental import pallas as pl
from jax.experimental.pallas import tpu as pltpu

EPS = 1e-6
LOG2E = 1.4426950408889634
LN2 = 0.6931471805599453

GLA_HEADS = 4
GLA_DK = 128
GLA_DV = 256
GLA_GATE_RANK = 16
GLA_GATE_TAU = 16.0
GLA_CHUNK = 64
GLA_SUB = 16
GLA_BLOCK = 512
GLA_FAST_CHUNK = 256
GLA_FAST_MAX_DECAY = 40.0

SWA_PATTERNS = ((128, 1), (512, 4), (2048, 16))
SWA_HEADS = 8
SWA_HEAD_DIM = 128
SWA_SPAN = 128
PERM_ROWS = 256
SWA_ATTN_ROWS = 512

LANES = 128
VMEM_LIMIT_BYTES = 56 * 1024 * 1024


def _cparams(semantics):
    return pltpu.CompilerParams(dimension_semantics=semantics,
                                vmem_limit_bytes=VMEM_LIMIT_BYTES)


def _rms_scale(x32):
    return lax.rsqrt(jnp.mean(x32 * x32, axis=-1, keepdims=True) + EPS)


def _perm_matrix(dil, to_natural):
    n = PERM_ROWS // dil
    row = lax.broadcasted_iota(jnp.int32, (PERM_ROWS, PERM_ROWS), 0)
    col = lax.broadcasted_iota(jnp.int32, (PERM_ROWS, PERM_ROWS), 1)
    tok, rm = (row, col) if to_natural else (col, row)
    rm_of_tok = (tok & (dil - 1)) * n + (tok >> (dil.bit_length() - 1))
    return jnp.where(rm == rm_of_tok, 1.0, 0.0).astype(jnp.bfloat16)


def _norm_rows(x32, g):
    return (x32 * _rms_scale(x32) * g).astype(jnp.bfloat16)


def _pipelined_tiles(norm_into, project_from):
    t = pl.program_id(0)

    @pl.when(t == 0)
    def _():
        norm_into(0)

    @pl.when((t > 0) & (t % 2 == 1))
    def _():
        norm_into(1)
        project_from(0)

    @pl.when((t > 0) & (t % 2 == 0))
    def _():
        norm_into(0)
        project_from(1)


def _swa_proj_kernel(x_ref, g_ref, w_ref, gqk_ref, o_ref, h_scr, *, dil):
    tm = x_ref.shape[0]
    n = PERM_ROWS // dil
    hd = SWA_HEADS * SWA_HEAD_DIM

    def norm_into(slot):
        h = _norm_rows(x_ref[...], g_ref[...])
        if dil == 1:
            h_scr[slot] = h
        else:
            perm = _perm_matrix(dil, to_natural=False)
            for sb in range(tm // PERM_ROWS):
                rows = slice(sb * PERM_ROWS, (sb + 1) * PERM_ROWS)
                h_scr[slot, rows, :] = jnp.dot(
                    perm, h[rows], preferred_element_type=jnp.float32).astype(jnp.bfloat16)

    def store(y_bf16, cols):
        if dil == 1:
            o_ref[0, 0, :, cols] = y_bf16
        else:
            for sb in range(tm // PERM_ROWS):
                for r in range(dil):
                    src = sb * PERM_ROWS + r * n
                    o_ref[0, r, sb * n:(sb + 1) * n, cols] = y_bf16[src:src + n]

    def project_from(slot):
        for j in range(3):
            y = jnp.dot(h_scr[slot], w_ref[:, j * hd:(j + 1) * hd],
                        preferred_element_type=jnp.float32)
            if j == 2:
                store(y.astype(jnp.bfloat16), slice(j * hd, (j + 1) * hd))
                continue
            gain = gqk_ref[j:j + 1, :]
            for h in range(SWA_HEADS):
                yh = y[:, h * SWA_HEAD_DIM:(h + 1) * SWA_HEAD_DIM]
                store((yh * _rms_scale(yh) * gain).astype(jnp.bfloat16),
                      slice(j * hd + h * SWA_HEAD_DIM, j * hd + (h + 1) * SWA_HEAD_DIM))

    _pipelined_tiles(norm_into, project_from)


def _swa_proj(x2d, g, w_qkv, gains_qk, *, layer, batch, seq, group, dil, tm):
    d = x2d.shape[1]
    gcols = 3 * SWA_HEADS * SWA_HEAD_DIM
    tiles_per_seq = seq // tm
    n_tiles = batch * tiles_per_seq

    def out_map(t):
        tile = jnp.maximum(t - 1, 0)
        return (tile // tiles_per_seq, 0, tile % tiles_per_seq, 0)

    return pl.pallas_call(
        functools.partial(_swa_proj_kernel, dil=dil),
        grid=(n_tiles + 1,),
        in_specs=[
            pl.BlockSpec((tm, d), lambda t: (jnp.minimum(t, n_tiles - 1), 0)),
            pl.BlockSpec((1, d), lambda t: (0, 0)),
            pl.BlockSpec((None, d, gcols), lambda t: (layer, 0, group)),
            pl.BlockSpec((2, SWA_HEAD_DIM), lambda t: (0, 0)),
        ],
        out_specs=pl.BlockSpec((1, dil, tm // dil, gcols), out_map),
        out_shape=jax.ShapeDtypeStruct((batch, dil, seq // dil, gcols), jnp.bfloat16),
        scratch_shapes=[pltpu.VMEM((2, tm, d), jnp.bfloat16)],
        compiler_params=_cparams(("arbitrary",)),
        name=f"swa_proj_d{dil}",
    )(x2d, g.reshape(1, d), w_qkv, gains_qk)


def _gla_in_kernel(x_ref, g_ref, w_ref, wz_ref, o_ref, z_ref, h_scr):
    def norm_into(slot):
        h_scr[slot] = _norm_rows(x_ref[...], g_ref[...])

    def project_from(slot):
        h = h_scr[slot]
        z_ref[...] = jnp.dot(h, wz_ref[...], preferred_element_type=jnp.float32)
        tn = GLA_HEADS * GLA_DV
        for j in range(w_ref.shape[1] // tn):
            cols = slice(j * tn, (j + 1) * tn)
            o_ref[:, cols] = jnp.dot(h, w_ref[:, cols],
                                     preferred_element_type=jnp.float32).astype(o_ref.dtype)

    _pipelined_tiles(norm_into, project_from)


def _gla_in_proj(x2d, g, w_main, w_z, *, tm):
    m, d = x2d.shape
    n = w_main.shape[1]
    nz = w_z.shape[1]
    n_tiles = m // tm
    cur = lambda t: (jnp.maximum(t - 1, 0), 0)
    return pl.pallas_call(
        _gla_in_kernel,
        grid=(n_tiles + 1,),
        in_specs=[
            pl.BlockSpec((tm, d), lambda t: (jnp.minimum(t, n_tiles - 1), 0)),
            pl.BlockSpec((1, d), lambda t: (0, 0)),
            pl.BlockSpec((d, n), lambda t: (0, 0)),
            pl.BlockSpec((d, nz), lambda t: (0, 0)),
        ],
        out_specs=[pl.BlockSpec((tm, n), cur), pl.BlockSpec((tm, nz), cur)],
        out_shape=[
            jax.ShapeDtypeStruct((m, n), jnp.bfloat16),
            jax.ShapeDtypeStruct((m, nz), jnp.float32),
        ],
        scratch_shapes=[pltpu.VMEM((2, tm, d), jnp.bfloat16)],
        compiler_params=_cparams(("arbitrary",)),
        name="gla_in_proj",
    )(x2d, g.reshape(1, d), w_main, w_z)


def _gla_out_kernel(o_ref, r_ref, go_ref, w_ref, x_ref, out_ref):
    parts = []
    for h in range(GLA_HEADS):
        vs = slice(h * GLA_DV, (h + 1) * GLA_DV)
        o = o_ref[:, vs].astype(jnp.float32)
        r = r_ref[:, vs].astype(jnp.float32)
        y = o * _rms_scale(o) * go_ref[:, vs]
        parts.append((y * (r / (1.0 + jnp.exp(-r)))).astype(jnp.bfloat16))
    y = jnp.concatenate(parts, axis=-1)
    out_ref[...] = x_ref[...] + jnp.dot(y, w_ref[...], preferred_element_type=jnp.float32)


def _gla_out(o2d, qkvr2d, g_out, w, x2d, *, layer, tm):
    m, d = x2d.shape
    hv = o2d.shape[1]
    r_block = qkvr2d.shape[1] // hv - 1
    return pl.pallas_call(
        _gla_out_kernel,
        grid=(m // tm,),
        in_specs=[
            pl.BlockSpec((tm, hv), lambda i: (i, 0)),
            pl.BlockSpec((tm, hv), lambda i: (i, r_block)),
            pl.BlockSpec((1, hv), lambda i: (0, 0)),
            pl.BlockSpec((None, hv, d), lambda i: (layer, 0, 0)),
            pl.BlockSpec((tm, d), lambda i: (i, 0)),
        ],
        out_specs=pl.BlockSpec((tm, d), lambda i: (i, 0)),
        out_shape=jax.ShapeDtypeStruct((m, d), jnp.float32),
        compiler_params=_cparams(("parallel",)),
        name="gla_out",
    )(o2d, qkvr2d, g_out, w, x2d)


def _mlp_kernel(x_ref, g_ref, wu_ref, wd_ref, out_ref, h_scr):
    i = pl.program_id(0)
    f = pl.program_id(1)
    last = pl.num_programs(1) - 1

    @pl.when((i == 0) & (f == 0))
    def _():
        h_scr[...] = _norm_rows(x_ref[...], g_ref[...])

    def hidden():
        u = jnp.dot(h_scr[...], wu_ref[...], preferred_element_type=jnp.float32)
        return jnp.square(jnp.maximum(u, 0.0)).astype(jnp.bfloat16)

    def down(a):
        return jnp.dot(a, wd_ref[...], preferred_element_type=jnp.float32)

    @pl.when(f == 0)
    def _():
        out_ref[...] = x_ref[...] + down(hidden())

    @pl.when((f > 0) & (f < last))
    def _():
        out_ref[...] += down(hidden())

    @pl.when(f == last)
    def _():
        a = hidden()
        h_scr[...] = _norm_rows(x_ref[...], g_ref[...])
        out_ref[...] += down(a)


def _mlp(x2d, g, w_up, w_down, *, layer, tm, tf):
    m, d = x2d.shape
    ff = w_up.shape[2]
    n_tiles = m // tm
    nf = ff // tf
    assert nf >= 2

    def x_map(i, f):
        return (jnp.minimum(i + f // (nf - 1), n_tiles - 1), 0)

    return pl.pallas_call(
        _mlp_kernel,
        grid=(n_tiles, nf),
        in_specs=[
            pl.BlockSpec((tm, d), x_map),
            pl.BlockSpec((1, d), lambda i, f: (0, 0)),
            pl.BlockSpec((None, d, tf), lambda i, f: (layer, 0, f)),
            pl.BlockSpec((None, tf, d), lambda i, f: (layer, f, 0)),
        ],
        out_specs=pl.BlockSpec((tm, d), lambda i, f: (i, 0)),
        out_shape=jax.ShapeDtypeStruct((m, d), jnp.float32),
        scratch_shapes=[pltpu.VMEM((tm, d), jnp.bfloat16)],
        compiler_params=_cparams(("arbitrary", "arbitrary")),
        name="sqrelu_mlp",
    )(x2d, g.reshape(1, d), w_up, w_down)


def _dot_nt(a, b):
    return lax.dot_general(a, b, (((1,), (1,)), ((), ())),
                           preferred_element_type=jnp.float32)


def _dot_tn(a, b):
    return lax.dot_general(a, b, (((0,), (0,)), ((), ())),
                           preferred_element_type=jnp.float32)


def _gla_kernel(q_ref, k_ref, v_ref, z_ref, wg_ref, bg_ref, o_ref,
                state_scr, b_scr, k_scr, brel_scr, decay_scr, *, blocks_per_seq):
    C, SUB, CF = GLA_CHUNK, GLA_SUB, GLA_FAST_CHUNK
    nsub = C // SUB
    nchunk = GLA_BLOCK // C
    t = pl.program_id(0)

    row = lax.broadcasted_iota(jnp.int32, (CF, CF), 0)
    col = lax.broadcasted_iota(jnp.int32, (CF, CF), 1)
    causal = row >= col
    tri = jnp.where(causal, 1.0, 0.0).astype(jnp.bfloat16)

    def gates(slot):
        zb = z_ref[0].astype(jnp.bfloat16)
        gp = jnp.dot(zb, wg_ref[...], preferred_element_type=jnp.float32) + bg_ref[...]
        g = (jnp.minimum(gp, 0.0) - jnp.log(1.0 + jnp.exp(-jnp.abs(gp)))) * (1.0 / GLA_GATE_TAU)
        g_hi = g.astype(jnp.bfloat16)
        g_lo = (g - g_hi.astype(jnp.float32)).astype(jnp.bfloat16)
        decay_max = jnp.zeros((1, GLA_DK), jnp.float32)
        for h in range(GLA_HEADS):
            ks = slice(h * GLA_DK, (h + 1) * GLA_DK)
            for cf in range(GLA_BLOCK // CF):
                rows = slice(cf * CF, (cf + 1) * CF)
                g_hl = jnp.concatenate([g_hi[rows, ks], g_lo[rows, ks]], axis=-1)
                b_hl = jnp.dot(tri, g_hl, preferred_element_type=jnp.float32)
                b = b_hl[:, :GLA_DK] + b_hl[:, GLA_DK:]
                b_scr[slot, h, rows, :] = b
                decay_max = jnp.maximum(decay_max, -b[CF - 1:CF, :])
        decay_scr[slot] = jnp.max(decay_max)

    def chunk_update(ci_rows, h, st, a, q_in, k, v, b):
        vs = slice(h * GLA_DV, (h + 1) * GLA_DV)
        b_last = b[b.shape[0] - 1:, :]
        o = (jnp.dot(a, v, preferred_element_type=jnp.float32)
             + _dot_nt(q_in, st.astype(jnp.bfloat16)))
        o_ref[0, ci_rows, vs] = o.astype(o_ref.dtype)
        k_dec = (k * jnp.exp(b_last - b)).astype(jnp.bfloat16)
        return st * jnp.exp(b_last) + _dot_tn(v, k_dec)

    def fast_path(slot, keep):
        for cf in range(GLA_BLOCK // CF):
            rows = slice(cf * CF, (cf + 1) * CF)
            for h in range(GLA_HEADS):
                ks = slice(h * GLA_DK, (h + 1) * GLA_DK)
                vs = slice(h * GLA_DV, (h + 1) * GLA_DV)
                b = b_scr[slot, h, rows, :]
                q = q_ref[0, rows, ks].astype(jnp.float32) * (GLA_DK ** -0.5)
                k = k_ref[0, rows, ks].astype(jnp.float32)
                v = v_ref[0, rows, vs]
                q_in = (q * jnp.exp(b)).astype(jnp.bfloat16)
                k_out = (k * jnp.exp(-b)).astype(jnp.bfloat16)
                a = jnp.where(causal, _dot_nt(q_in, k_out), 0.0).astype(jnp.bfloat16)
                st = state_scr[h] * keep if cf == 0 else state_scr[h]
                state_scr[h] = chunk_update(rows, h, st, a, q_in, k, v, b)

    def robust_path(slot, keep):
        srow = lax.broadcasted_iota(jnp.int32, (SUB, C), 0)
        scol = lax.broadcasted_iota(jnp.int32, (SUB, C), 1)

        def chunk_body(ci, carry):
            rows = pl.ds(pl.multiple_of(ci * C, C), C)
            keep_c = jnp.where(ci == 0, keep, 1.0)
            for h in range(GLA_HEADS):
                ks = slice(h * GLA_DK, (h + 1) * GLA_DK)
                vs = slice(h * GLA_DV, (h + 1) * GLA_DV)
                c0 = ci * C
                before = b_scr[slot, h, pl.ds(jnp.maximum(c0 - 1, 0), 1), :]
                base = jnp.where(c0 % CF == 0, 0.0, before)
                b = b_scr[slot, h, rows, :] - base
                q = q_ref[0, rows, ks].astype(jnp.float32) * (GLA_DK ** -0.5)
                k = k_ref[0, rows, ks].astype(jnp.float32)
                v = v_ref[0, rows, vs]
                k_scr[h] = k
                brel_scr[h] = b

                refs = [jnp.zeros((1, GLA_DK), jnp.float32)]
                for i in range(1, nsub):
                    refs.append(brel_scr[h, i * SUB - 1:i * SUB, :])
                ref_rows = jnp.concatenate(
                    [jnp.broadcast_to(r, (SUB, GLA_DK)) for r in refs], axis=0)
                qe = (q * jnp.exp(b - ref_rows)).astype(jnp.bfloat16)

                a_rows = []
                for i in range(nsub):
                    r0 = i * SUB
                    qi = q[r0:r0 + SUB]
                    bi = b[r0:r0 + SUB]
                    key = jnp.where(srow >= scol - r0, scol, -1)
                    blk = jnp.zeros((SUB, C), jnp.float32)
                    for jj in range(SUB):
                        j = r0 + jj
                        bj = brel_scr[h, j:j + 1, :]
                        kj = k_scr[h, j:j + 1, :]
                        e = jnp.exp(jnp.minimum(bi - bj, 0.0))
                        s = jnp.sum(qi * e * kj, axis=-1, keepdims=True)
                        blk = jnp.where(key == j, s, blk)
                    if i > 0:
                        ke = (k * jnp.exp(jnp.minimum(refs[i] - b, 0.0))).astype(jnp.bfloat16)
                        off = _dot_nt(qe[r0:r0 + SUB], ke)
                        blk = jnp.where(scol < r0, off, blk)
                    a_rows.append(blk)
                a = jnp.concatenate(a_rows, axis=0).astype(jnp.bfloat16)
                q_in = (q * jnp.exp(b)).astype(jnp.bfloat16)
                state_scr[h] = chunk_update(rows, h, state_scr[h] * keep_c, a, q_in, k, v, b)
            return carry

        lax.fori_loop(0, nchunk, chunk_body, 0)

    def step(cur, nxt):
        keep = jnp.where((t - 1) % blocks_per_seq == 0, 0.0, 1.0)
        mild = decay_scr[cur] <= GLA_FAST_MAX_DECAY

        @pl.when(mild)
        def _():
            gates(nxt)
            fast_path(cur, keep)

        @pl.when(jnp.logical_not(mild))
        def _():
            gates(nxt)
            robust_path(cur, keep)

    @pl.when(t == 0)
    def _():
        state_scr[...] = jnp.zeros_like(state_scr)
        gates(0)

    @pl.when((t > 0) & (t % 2 == 1))
    def _():
        step(0, 1)

    @pl.when((t > 0) & (t % 2 == 0))
    def _():
        step(1, 0)


def _gla_core(qkvr, z, w_gate, b_gate, *, batch, seq):
    hk = GLA_HEADS * GLA_DK
    hv = GLA_HEADS * GLA_DV
    tb = GLA_BLOCK
    bps = seq // tb
    n_blocks = batch * bps

    def prev(col):
        def index_map(t):
            blk = jnp.maximum(t - 1, 0)
            return (blk // bps, blk % bps, col)
        return index_map

    def z_map(t):
        blk = jnp.minimum(t, n_blocks - 1)
        return (blk // bps, blk % bps, 0)

    return pl.pallas_call(
        functools.partial(_gla_kernel, blocks_per_seq=bps),
        grid=(n_blocks + 1,),
        in_specs=[
            pl.BlockSpec((1, tb, hk), prev(0)),
            pl.BlockSpec((1, tb, hk), prev(1)),
            pl.BlockSpec((1, tb, hv), prev(1)),
            pl.BlockSpec((1, tb, LANES), z_map),
            pl.BlockSpec((LANES, hk), lambda t: (0, 0)),
            pl.BlockSpec((1, hk), lambda t: (0, 0)),
        ],
        out_specs=pl.BlockSpec((1, tb, hv), prev(0)),
        out_shape=jax.ShapeDtypeStruct((batch, seq, hv), jnp.bfloat16),
        scratch_shapes=[
            pltpu.VMEM((GLA_HEADS, GLA_DV, GLA_DK), jnp.float32),
            pltpu.VMEM((2, GLA_HEADS, GLA_BLOCK, GLA_DK), jnp.float32),
            pltpu.VMEM((GLA_HEADS, GLA_CHUNK, GLA_DK), jnp.float32),
            pltpu.VMEM((GLA_HEADS, GLA_CHUNK, GLA_DK), jnp.float32),
            pltpu.SMEM((2,), jnp.float32),
        ],
        compiler_params=_cparams(("arbitrary",)),
        name="gla_core",
    )(qkvr, qkvr, qkvr, z, w_gate, b_gate)


def _swa_kernel(q_ref, k_ref, v_ref, o_ref, lse_ref, kprev_scr, vprev_scr, *, nsub):
    blk = SWA_SPAN
    first = pl.program_id(2) == 0

    @pl.when(first)
    def _():
        kprev_scr[...] = jnp.zeros_like(kprev_scr)
        vprev_scr[...] = jnp.zeros_like(vprev_scr)

    qi = lax.broadcasted_iota(jnp.int32, (blk, 2 * blk), 0)
    kc = lax.broadcasted_iota(jnp.int32, (blk, 2 * blk), 1)
    band = (kc >= qi) & (kc <= qi + blk)
    lo = jnp.where(first, blk, 0)
    band_first = (kc >= jnp.maximum(qi, lo)) & (kc <= qi + blk)
    lane = lax.broadcasted_iota(jnp.int32, (blk, LANES), 1)
    for r in range(q_ref.shape[1]):
        for s in range(nsub):
            rows = slice(s * blk, (s + 1) * blk)
            valid = band_first if s == 0 else band
            lse_tile = jnp.zeros((blk, LANES), jnp.float32)
            for h in range(SWA_HEADS):
                hs = slice(h * SWA_HEAD_DIM, (h + 1) * SWA_HEAD_DIM)
                q = q_ref[0, r, rows, hs]
                if s == 0:
                    k = jnp.concatenate([kprev_scr[r, :, hs], k_ref[0, r, rows, hs]], axis=0)
                    v = jnp.concatenate([vprev_scr[r, :, hs], v_ref[0, r, rows, hs]], axis=0)
                else:
                    k = k_ref[0, r, (s - 1) * blk:(s + 1) * blk, hs]
                    v = v_ref[0, r, (s - 1) * blk:(s + 1) * blk, hs]
                sc = jnp.where(valid, _dot_nt(q, k), -jnp.inf)
                m = jnp.max(sc, axis=-1, keepdims=True)
                p = jnp.exp2(sc - m)
                l = jnp.sum(p, axis=-1, keepdims=True)
                o = jnp.dot(p.astype(jnp.bfloat16), v, preferred_element_type=jnp.float32) / l
                o_ref[0, r, rows, hs] = o.astype(o_ref.dtype)
                lse_tile = jnp.where(lane == h, (m + jnp.log2(l)) * LN2, lse_tile)
            lse_ref[0, r, rows, :] = lse_tile
        kprev_scr[r] = k_ref[0, r, (nsub - 1) * blk:nsub * blk, :]
        vprev_scr[r] = v_ref[0, r, (nsub - 1) * blk:nsub * blk, :]


def _swa_group(qkv, *, dil, rows, subseqs):
    batch, _, sub_len, _ = qkv.shape
    hd = SWA_HEADS * SWA_HEAD_DIM
    col = lambda c: (lambda b, r, j: (b, r, j, c))
    return pl.pallas_call(
        functools.partial(_swa_kernel, nsub=rows // SWA_SPAN),
        grid=(batch, dil // subseqs, sub_len // rows),
        in_specs=[pl.BlockSpec((1, subseqs, rows, hd), col(0)),
                  pl.BlockSpec((1, subseqs, rows, hd), col(1)),
                  pl.BlockSpec((1, subseqs, rows, hd), col(2))],
        out_specs=[pl.BlockSpec((1, subseqs, rows, hd), col(0)),
                   pl.BlockSpec((1, subseqs, rows, LANES), col(0))],
        out_shape=[jax.ShapeDtypeStruct((batch, dil, sub_len, hd), jnp.bfloat16),
                   jax.ShapeDtypeStruct((batch, dil, sub_len, LANES), jnp.float32)],
        scratch_shapes=[pltpu.VMEM((subseqs, SWA_SPAN, hd), jnp.bfloat16),
                        pltpu.VMEM((subseqs, SWA_SPAN, hd), jnp.bfloat16)],
        compiler_params=_cparams(("parallel", "parallel", "arbitrary")),
        name=f"swa_attn_d{dil}",
    )(qkv, qkv, qkv)


def _swa_out_kernel(o1_ref, o2_ref, o3_ref, l1_ref, l2_ref, l3_ref, w_ref, x_ref, out_ref):
    tm = x_ref.shape[1]

    def rows_of(ref, dil, sb):
        n = PERM_ROWS // dil
        if dil == 1:
            return ref[0, 0, sb * PERM_ROWS:(sb + 1) * PERM_ROWS, :]
        return jnp.concatenate([ref[0, r, sb * n:(sb + 1) * n, :] for r in range(dil)], axis=0)

    def natural_bf16(ref, dil, sb):
        rm = rows_of(ref, dil, sb)
        if dil == 1:
            return rm.astype(jnp.float32)
        return jnp.dot(_perm_matrix(dil, to_natural=True), rm,
                       preferred_element_type=jnp.float32)

    def natural_f32(ref, dil, sb):
        rm = rows_of(ref, dil, sb)
        if dil == 1:
            return rm
        perm = _perm_matrix(dil, to_natural=True)
        hi = rm.astype(jnp.bfloat16)
        lo = (rm - hi.astype(jnp.float32)).astype(jnp.bfloat16)
        return (jnp.dot(perm, hi, preferred_element_type=jnp.float32)
                + jnp.dot(perm, lo, preferred_element_type=jnp.float32))

    dils = [d for _, d in SWA_PATTERNS]
    merged = []
    for sb in range(tm // PERM_ROWS):
        l1, l2, l3 = (natural_f32(r, d, sb) for r, d in zip((l1_ref, l2_ref, l3_ref), dils))
        o1, o2, o3 = (natural_bf16(r, d, sb) for r, d in zip((o1_ref, o2_ref, o3_ref), dils))
        mx = jnp.maximum(jnp.maximum(l1, l2), l3)
        e1, e2, e3 = jnp.exp(l1 - mx), jnp.exp(l2 - mx), jnp.exp(l3 - mx)
        inv = 1.0 / (e1 + e2 + e3)
        w2, w3 = e2 * inv, e3 * inv
        parts = []
        for h in range(SWA_HEADS):
            hs = slice(h * SWA_HEAD_DIM, (h + 1) * SWA_HEAD_DIM)
            base = o1[:, hs]
            parts.append((base + w2[:, h:h + 1] * (o2[:, hs] - base)
                          + w3[:, h:h + 1] * (o3[:, hs] - base)).astype(jnp.bfloat16))
        merged.append(jnp.concatenate(parts, axis=-1))
    o = jnp.concatenate(merged, axis=0)
    out_ref[0] = x_ref[0] + jnp.dot(o, w_ref[...], preferred_element_type=jnp.float32)


def _swa_out(outs, lses, w, x3d, *, layer, tm):
    batch, seq, d = x3d.shape
    hd = w.shape[1]
    assert tm % PERM_ROWS == 0

    def group_tile(dil, n):
        return pl.BlockSpec((1, dil, tm // dil, n), lambda b, i: (b, 0, i, 0))

    dils = [dil for _, dil in SWA_PATTERNS]
    return pl.pallas_call(
        _swa_out_kernel,
        grid=(batch, seq // tm),
        in_specs=[group_tile(dil, hd) for dil in dils] + [group_tile(dil, LANES) for dil in dils]
        + [pl.BlockSpec((None, hd, d), lambda b, i: (layer, 0, 0)),
           pl.BlockSpec((1, tm, d), lambda b, i: (b, i, 0))],
        out_specs=pl.BlockSpec((1, tm, d), lambda b, i: (b, i, 0)),
        out_shape=jax.ShapeDtypeStruct((batch, seq, d), jnp.float32),
        compiler_params=_cparams(("parallel", "parallel")),
        name="swa_merge_out",
    )(*outs, *lses, w, x3d)


def kernel(x, norm_mix, norm_mlp, gla_w_in, gla_w_gate_up, gla_b_gate, gla_g_out, gla_w_out,
           swa_w_qkv, swa_g_q, swa_g_k, swa_w_out, mlp_w_up, mlp_w_down):
    batch, seq, d = x.shape
    depth = norm_mix.shape[0]
    hk = GLA_HEADS * GLA_DK
    hv = GLA_HEADS * GLA_DV
    n_main = 2 * hk + 2 * hv
    bf = jnp.bfloat16
    assert seq % GLA_BLOCK == 0
    for window, dilation in SWA_PATTERNS:
        assert seq % window == 0

    gla_w_out_bf, swa_w_qkv_bf, swa_w_out_bf = (w.astype(bf) for w in
                                                (gla_w_out, swa_w_qkv, swa_w_out))
    mlp_w_up_bf, mlp_w_down_bf = mlp_w_up.astype(bf), mlp_w_down.astype(bf)

    x2d = x.reshape(batch * seq, d)
    for i in range(depth):
        j = i // 2
        if i % 2 == 0:
            w_in = gla_w_in[j]
            w_z = jnp.pad(w_in[:, n_main:], ((0, 0), (0, LANES - GLA_GATE_RANK))).astype(bf)
            qkvr, z = _gla_in_proj(x2d, norm_mix[i], w_in[:, :n_main].astype(bf), w_z, tm=1024)
            w_gate = jnp.pad(gla_w_gate_up[j], ((0, LANES - GLA_GATE_RANK), (0, 0))).astype(bf)
            o = _gla_core(qkvr.reshape(batch, seq, n_main), z.reshape(batch, seq, LANES),
                          w_gate, gla_b_gate[j].reshape(1, hk), batch=batch, seq=seq)
            x2d = _gla_out(o.reshape(batch * seq, hv), qkvr, gla_g_out[j].reshape(1, hv),
                           gla_w_out_bf, x2d, layer=j, tm=1024)
        else:
            outs, lses = [], []
            for gi, (window, dil) in enumerate(SWA_PATTERNS):
                assert window // dil == SWA_SPAN
                gains = jnp.stack([swa_g_q[j, gi] * (SWA_HEAD_DIM ** -0.5 * LOG2E), swa_g_k[j, gi]])
                qkv = _swa_proj(x2d, norm_mix[i], swa_w_qkv_bf, gains, layer=j, batch=batch,
                                seq=seq, group=gi, dil=dil, tm=1024)
                rows = min(SWA_ATTN_ROWS, seq // dil)
                o, lse = _swa_group(qkv, dil=dil, rows=rows, subseqs=SWA_ATTN_ROWS // rows)
                outs.append(o)
                lses.append(lse)
            x2d = _swa_out(outs, lses, swa_w_out_bf, x2d.reshape(batch, seq, d),
                           layer=j, tm=512).reshape(batch * seq, d)
        x2d = _mlp(x2d, norm_mlp[i], mlp_w_up_bf, mlp_w_down_bf, layer=i, tm=1024, tf=2048)
    return x2d.reshape(batch, seq, d)
```

```python
import functools

import jax
import jax.numpy as jnp
from jax import lax
from jax.experimental import pallas as pl
from jax.experimental.pallas import tpu as pltpu

EPS = 1e-6
LOG2E = 1.4426950408889634
LN2 = 0.6931471805599453

GLA_HEADS = 4
GLA_DK = 128
GLA_DV = 256
GLA_GATE_RANK = 16
GLA_GATE_TAU = 16.0
GLA_CHUNK = 64
GLA_SUB = 16
GLA_BLOCK = 512
GLA_FAST_CHUNK = 256
GLA_FAST_MAX_DECAY = 40.0

SWA_PATTERNS = ((128, 1), (512, 4), (2048, 16))
SWA_HEADS = 8
SWA_HEAD_DIM = 128
SWA_SPAN = 128
PERM_ROWS = 256
SWA_ATTN_ROWS = 512

LANES = 128
VMEM_LIMIT_BYTES = 56 * 1024 * 1024


def _cparams(semantics):
    return pltpu.CompilerParams(dimension_semantics=semantics,
                                vmem_limit_bytes=VMEM_LIMIT_BYTES)


def _rms_scale(x32):
    return lax.rsqrt(jnp.mean(x32 * x32, axis=-1, keepdims=True) + EPS)


def _perm_matrix(dil, to_natural):
    n = PERM_ROWS // dil
    row = lax.broadcasted_iota(jnp.int32, (PERM_ROWS, PERM_ROWS), 0)
    col = lax.broadcasted_iota(jnp.int32, (PERM_ROWS, PERM_ROWS), 1)
    tok, rm = (row, col) if to_natural else (col, row)
    rm_of_tok = (tok & (dil - 1)) * n + (tok >> (dil.bit_length() - 1))
    return jnp.where(rm == rm_of_tok, 1.0, 0.0).astype(jnp.bfloat16)


def _norm_rows(x32, g):
    return (x32 * _rms_scale(x32) * g).astype(jnp.bfloat16)


def _pipelined_tiles(norm_into, project_from):
    t = pl.program_id(0)

    @pl.when(t == 0)
    def _():
        norm_into(0)

    @pl.when((t > 0) & (t % 2 == 1))
    def _():
        norm_into(1)
        project_from(0)

    @pl.when((t > 0) & (t % 2 == 0))
    def _():
        norm_into(0)
        project_from(1)


def _swa_proj_kernel(x_ref, g_ref, w_ref, gqk_ref, o_ref, h_scr, *, dil):
    tm = x_ref.shape[0]
    n = PERM_ROWS // dil
    hd = SWA_HEADS * SWA_HEAD_DIM

    def norm_into(slot):
        h = _norm_rows(x_ref[...], g_ref[...])
        if dil == 1:
            h_scr[slot] = h
        else:
            perm = _perm_matrix(dil, to_natural=False)
            for sb in range(tm // PERM_ROWS):
                rows = slice(sb * PERM_ROWS, (sb + 1) * PERM_ROWS)
                h_scr[slot, rows, :] = jnp.dot(
                    perm, h[rows], preferred_element_type=jnp.float32).astype(jnp.bfloat16)

    def store(y_bf16, cols):
        if dil == 1:
            o_ref[0, 0, :, cols] = y_bf16
        else:
            for sb in range(tm // PERM_ROWS):
                for r in range(dil):
                    src = sb * PERM_ROWS + r * n
                    o_ref[0, r, sb * n:(sb + 1) * n, cols] = y_bf16[src:src + n]

    def project_from(slot):
        for j in range(3):
            y = jnp.dot(h_scr[slot], w_ref[:, j * hd:(j + 1) * hd],
                        preferred_element_type=jnp.float32)
            if j == 2:
                store(y.astype(jnp.bfloat16), slice(j * hd, (j + 1) * hd))
                continue
            gain = gqk_ref[j:j + 1, :]
            for h in range(SWA_HEADS):
                yh = y[:, h * SWA_HEAD_DIM:(h + 1) * SWA_HEAD_DIM]
                store((yh * _rms_scale(yh) * gain).astype(jnp.bfloat16),
                      slice(j * hd + h * SWA_HEAD_DIM, j * hd + (h + 1) * SWA_HEAD_DIM))

    _pipelined_tiles(norm_into, project_from)


def _swa_proj(x2d, g, w_qkv, gains_qk, *, layer, batch, seq, group, dil, tm):
    d = x2d.shape[1]
    gcols = 3 * SWA_HEADS * SWA_HEAD_DIM
    tiles_per_seq = seq // tm
    n_tiles = batch * tiles_per_seq

    def out_map(t):
        tile = jnp.maximum(t - 1, 0)
        return (tile // tiles_per_seq, 0, tile % tiles_per_seq, 0)

    return pl.pallas_call(
        functools.partial(_swa_proj_kernel, dil=dil),
        grid=(n_tiles + 1,),
        in_specs=[
            pl.BlockSpec((tm, d), lambda t: (jnp.minimum(t, n_tiles - 1), 0)),
            pl.BlockSpec((1, d), lambda t: (0, 0)),
            pl.BlockSpec((None, d, gcols), lambda t: (layer, 0, group)),
            pl.BlockSpec((2, SWA_HEAD_DIM), lambda t: (0, 0)),
        ],
        out_specs=pl.BlockSpec((1, dil, tm // dil, gcols), out_map),
        out_shape=jax.ShapeDtypeStruct((batch, dil, seq // dil, gcols), jnp.bfloat16),
        scratch_shapes=[pltpu.VMEM((2, tm, d), jnp.bfloat16)],
        compiler_params=_cparams(("arbitrary",)),
        name=f"swa_proj_d{dil}",
    )(x2d, g.reshape(1, d), w_qkv, gains_qk)


def _gla_in_kernel(x_ref, g_ref, w_ref, wz_ref, o_ref, z_ref, h_scr):
    def norm_into(slot):
        h_scr[slot] = _norm_rows(x_ref[...], g_ref[...])

    def project_from(slot):
        h = h_scr[slot]
        z_ref[...] = jnp.dot(h, wz_ref[...], preferred_element_type=jnp.float32)
        tn = GLA_HEADS * GLA_DV
        for j in range(w_ref.shape[1] // tn):
            cols = slice(j * tn, (j + 1) * tn)
            o_ref[:, cols] = jnp.dot(h, w_ref[:, cols],
                                     preferred_element_type=jnp.float32).astype(o_ref.dtype)

    _pipelined_tiles(norm_into, project_from)


def _gla_in_proj(x2d, g, w_main, w_z, *, tm):
    m, d = x2d.shape
    n = w_main.shape[1]
    nz = w_z.shape[1]
    n_tiles = m // tm
    cur = lambda t: (jnp.maximum(t - 1, 0), 0)
    return pl.pallas_call(
        _gla_in_kernel,
        grid=(n_tiles + 1,),
        in_specs=[
            pl.BlockSpec((tm, d), lambda t: (jnp.minimum(t, n_tiles - 1), 0)),
            pl.BlockSpec((1, d), lambda t: (0, 0)),
            pl.BlockSpec((d, n), lambda t: (0, 0)),
            pl.BlockSpec((d, nz), lambda t: (0, 0)),
        ],
        out_specs=[pl.BlockSpec((tm, n), cur), pl.BlockSpec((tm, nz), cur)],
        out_shape=[
            jax.ShapeDtypeStruct((m, n), jnp.bfloat16),
            jax.ShapeDtypeStruct((m, nz), jnp.float32),
        ],
        scratch_shapes=[pltpu.VMEM((2, tm, d), jnp.bfloat16)],
        compiler_params=_cparams(("arbitrary",)),
        name="gla_in_proj",
    )(x2d, g.reshape(1, d), w_main, w_z)


def _mlp_kernel(x_ref, xn_ref, g_ref, wu_ref, wd_ref, out_ref, h_scr):
    @pl.when(pl.program_id(0) == 0)
    def _():
        h_scr[...] = _norm_rows(x_ref[...], g_ref[...])

    u = jnp.dot(h_scr[...], wu_ref[...], preferred_element_type=jnp.float32)
    a = jnp.square(jnp.maximum(u, 0.0)).astype(jnp.bfloat16)
    h_scr[...] = _norm_rows(xn_ref[...], g_ref[...])
    out_ref[...] = x_ref[...] + jnp.dot(a, wd_ref[...], preferred_element_type=jnp.float32)


def _mlp(x2d, g, w_up, w_down, *, layer, tm):
    m, d = x2d.shape
    ff = w_up.shape[2]
    n_tiles = m // tm
    resident = pl.Buffered(1)
    return pl.pallas_call(
        _mlp_kernel,
        grid=(n_tiles,),
        in_specs=[
            pl.BlockSpec((tm, d), lambda i: (i, 0)),
            pl.BlockSpec((tm, d), lambda i: (jnp.minimum(i + 1, n_tiles - 1), 0)),
            pl.BlockSpec((1, d), lambda i: (0, 0)),
            pl.BlockSpec((None, d, ff), lambda i: (layer, 0, 0), pipeline_mode=resident),
            pl.BlockSpec((None, ff, d), lambda i: (layer, 0, 0), pipeline_mode=resident),
        ],
        out_specs=pl.BlockSpec((tm, d), lambda i: (i, 0)),
        out_shape=jax.ShapeDtypeStruct((m, d), jnp.float32),
        scratch_shapes=[pltpu.VMEM((tm, d), jnp.bfloat16)],
        compiler_params=_cparams(("arbitrary",)),
        name="sqrelu_mlp",
    )(x2d, x2d, g.reshape(1, d), w_up, w_down)


def _dot_nt(a, b):
    return lax.dot_general(a, b, (((1,), (1,)), ((), ())),
                           preferred_element_type=jnp.float32)


def _dot_tn(a, b):
    return lax.dot_general(a, b, (((0,), (0,)), ((), ())),
                           preferred_element_type=jnp.float32)


def _gla_kernel(q_ref, k_ref, v_ref, z_ref, wg_ref, bg_ref, r_ref, go_ref, wo_ref, x_ref,
                out_ref, state_scr, b_scr, k_scr, brel_scr, o_scr, decay_scr, *, blocks_per_seq):
    C, SUB, CF = GLA_CHUNK, GLA_SUB, GLA_FAST_CHUNK
    nsub = C // SUB
    nchunk = GLA_BLOCK // C
    t = pl.program_id(0)

    row = lax.broadcasted_iota(jnp.int32, (CF, CF), 0)
    col = lax.broadcasted_iota(jnp.int32, (CF, CF), 1)
    causal = row >= col
    tri = jnp.where(causal, 1.0, 0.0).astype(jnp.bfloat16)

    def gates(slot):
        zb = z_ref[0].astype(jnp.bfloat16)
        gp = jnp.dot(zb, wg_ref[...], preferred_element_type=jnp.float32) + bg_ref[...]
        g = (jnp.minimum(gp, 0.0) - jnp.log(1.0 + jnp.exp(-jnp.abs(gp)))) * (1.0 / GLA_GATE_TAU)
        g_hi = g.astype(jnp.bfloat16)
        g_lo = (g - g_hi.astype(jnp.float32)).astype(jnp.bfloat16)
        decay_max = jnp.zeros((1, GLA_DK), jnp.float32)
        for h in range(GLA_HEADS):
            ks = slice(h * GLA_DK, (h + 1) * GLA_DK)
            for cf in range(GLA_BLOCK // CF):
                rows = slice(cf * CF, (cf + 1) * CF)
                g_hl = jnp.concatenate([g_hi[rows, ks], g_lo[rows, ks]], axis=-1)
                b_hl = jnp.dot(tri, g_hl, preferred_element_type=jnp.float32)
                b = b_hl[:, :GLA_DK] + b_hl[:, GLA_DK:]
                b_scr[slot, h, rows, :] = b
                decay_max = jnp.maximum(decay_max, -b[CF - 1:CF, :])
        decay_scr[slot] = jnp.max(decay_max)

    def chunk_update(slot, ci_rows, h, st, a, q_in, k, v, b):
        vs = slice(h * GLA_DV, (h + 1) * GLA_DV)
        b_last = b[b.shape[0] - 1:, :]
        o = (jnp.dot(a, v, preferred_element_type=jnp.float32)
             + _dot_nt(q_in, st.astype(jnp.bfloat16)))
        o_scr[slot, ci_rows, vs] = o.astype(o_scr.dtype)
        k_dec = (k * jnp.exp(b_last - b)).astype(jnp.bfloat16)
        return st * jnp.exp(b_last) + _dot_tn(v, k_dec)

    def fast_path(slot, keep):
        for cf in range(GLA_BLOCK // CF):
            rows = slice(cf * CF, (cf + 1) * CF)
            for h in range(GLA_HEADS):
                ks = slice(h * GLA_DK, (h + 1) * GLA_DK)
                vs = slice(h * GLA_DV, (h + 1) * GLA_DV)
                b = b_scr[slot, h, rows, :]
                q = q_ref[0, rows, ks].astype(jnp.float32) * (GLA_DK ** -0.5)
                k = k_ref[0, rows, ks].astype(jnp.float32)
                v = v_ref[0, rows, vs]
                q_in = (q * jnp.exp(b)).astype(jnp.bfloat16)
                k_out = (k * jnp.exp(-b)).astype(jnp.bfloat16)
                a = jnp.where(causal, _dot_nt(q_in, k_out), 0.0).astype(jnp.bfloat16)
                st = state_scr[h] * keep if cf == 0 else state_scr[h]
                state_scr[h] = chunk_update(slot, rows, h, st, a, q_in, k, v, b)

    def robust_path(slot, keep):
        srow = lax.broadcasted_iota(jnp.int32, (SUB, C), 0)
        scol = lax.broadcasted_iota(jnp.int32, (SUB, C), 1)

        def chunk_body(ci, carry):
            rows = pl.ds(pl.multiple_of(ci * C, C), C)
            keep_c = jnp.where(ci == 0, keep, 1.0)
            for h in range(GLA_HEADS):
                ks = slice(h * GLA_DK, (h + 1) * GLA_DK)
                vs = slice(h * GLA_DV, (h + 1) * GLA_DV)
                c0 = ci * C
                before = b_scr[slot, h, pl.ds(jnp.maximum(c0 - 1, 0), 1), :]
                base = jnp.where(c0 % CF == 0, 0.0, before)
                b = b_scr[slot, h, rows, :] - base
                q = q_ref[0, rows, ks].astype(jnp.float32) * (GLA_DK ** -0.5)
                k = k_ref[0, rows, ks].astype(jnp.float32)
                v = v_ref[0, rows, vs]
                k_scr[h] = k
                brel_scr[h] = b

                refs = [jnp.zeros((1, GLA_DK), jnp.float32)]
                for i in range(1, nsub):
                    refs.append(brel_scr[h, i * SUB - 1:i * SUB, :])
                ref_rows = jnp.concatenate(
                    [jnp.broadcast_to(r, (SUB, GLA_DK)) for r in refs], axis=0)
                qe = (q * jnp.exp(b - ref_rows)).astype(jnp.bfloat16)

                a_rows = []
                for i in range(nsub):
                    r0 = i * SUB
                    qi = q[r0:r0 + SUB]
                    bi = b[r0:r0 + SUB]
                    key = jnp.where(srow >= scol - r0, scol, -1)
                    blk = jnp.zeros((SUB, C), jnp.float32)
                    for jj in range(SUB):
                        j = r0 + jj
                        bj = brel_scr[h, j:j + 1, :]
                        kj = k_scr[h, j:j + 1, :]
                        e = jnp.exp(jnp.minimum(bi - bj, 0.0))
                        s = jnp.sum(qi * e * kj, axis=-1, keepdims=True)
                        blk = jnp.where(key == j, s, blk)
                    if i > 0:
                        ke = (k * jnp.exp(jnp.minimum(refs[i] - b, 0.0))).astype(jnp.bfloat16)
                        off = _dot_nt(qe[r0:r0 + SUB], ke)
                        blk = jnp.where(scol < r0, off, blk)
                    a_rows.append(blk)
                a = jnp.concatenate(a_rows, axis=0).astype(jnp.bfloat16)
                q_in = (q * jnp.exp(b)).astype(jnp.bfloat16)
                state_scr[h] = chunk_update(slot, rows, h, state_scr[h] * keep_c,
                                            a, q_in, k, v, b)
            return carry

        lax.fori_loop(0, nchunk, chunk_body, 0)

    def output_stage(slot):
        parts = []
        for h in range(GLA_HEADS):
            vs = slice(h * GLA_DV, (h + 1) * GLA_DV)
            o = o_scr[slot, :, vs].astype(jnp.float32)
            r = r_ref[0, :, vs].astype(jnp.float32)
            y = o * _rms_scale(o) * go_ref[:, vs]
            parts.append((y * (r / (1.0 + jnp.exp(-r)))).astype(jnp.bfloat16))
        y = jnp.concatenate(parts, axis=-1)
        out_ref[0] = x_ref[0] + jnp.dot(y, wo_ref[...], preferred_element_type=jnp.float32)

    def step(cur, nxt):
        keep = jnp.where((t - 1) % blocks_per_seq == 0, 0.0, 1.0)
        mild = decay_scr[cur] <= GLA_FAST_MAX_DECAY

        @pl.when(mild)
        def _():
            gates(nxt)
            fast_path(cur, keep)
            output_stage(nxt)

        @pl.when(jnp.logical_not(mild))
        def _():
            gates(nxt)
            robust_path(cur, keep)
            output_stage(nxt)

    @pl.when(t == 0)
    def _():
        state_scr[...] = jnp.zeros_like(state_scr)
        o_scr[...] = jnp.zeros_like(o_scr)
        gates(0)

    @pl.when((t > 0) & (t % 2 == 1))
    def _():
        step(0, 1)

    @pl.when((t > 0) & (t % 2 == 0))
    def _():
        step(1, 0)


def _gla_core(qkvr, z, w_gate, b_gate, g_out, w_out, x3d, *, layer):
    batch, seq, d = x3d.shape
    hk = GLA_HEADS * GLA_DK
    hv = GLA_HEADS * GLA_DV
    tb = GLA_BLOCK
    bps = seq // tb
    n_blocks = batch * bps

    def lagged(lag, col):
        def index_map(t):
            blk = jnp.clip(t - lag, 0, n_blocks - 1)
            return (blk // bps, blk % bps, col)
        return index_map

    return pl.pallas_call(
        functools.partial(_gla_kernel, blocks_per_seq=bps),
        grid=(n_blocks + 2,),
        in_specs=[
            pl.BlockSpec((1, tb, hk), lagged(1, 0)),
            pl.BlockSpec((1, tb, hk), lagged(1, 1)),
            pl.BlockSpec((1, tb, hv), lagged(1, 1)),
            pl.BlockSpec((1, tb, LANES), lagged(0, 0)),
            pl.BlockSpec((LANES, hk), lambda t: (0, 0)),
            pl.BlockSpec((1, hk), lambda t: (0, 0)),
            pl.BlockSpec((1, tb, hv), lagged(2, 2)),
            pl.BlockSpec((1, hv), lambda t: (0, 0)),
            pl.BlockSpec((None, hv, d), lambda t: (layer, 0, 0)),
            pl.BlockSpec((1, tb, d), lagged(2, 0)),
        ],
        out_specs=pl.BlockSpec((1, tb, d), lagged(2, 0)),
        out_shape=jax.ShapeDtypeStruct((batch, seq, d), jnp.float32),
        scratch_shapes=[
            pltpu.VMEM((GLA_HEADS, GLA_DV, GLA_DK), jnp.float32),
            pltpu.VMEM((2, GLA_HEADS, GLA_BLOCK, GLA_DK), jnp.float32),
            pltpu.VMEM((GLA_HEADS, GLA_CHUNK, GLA_DK), jnp.float32),
            pltpu.VMEM((GLA_HEADS, GLA_CHUNK, GLA_DK), jnp.float32),
            pltpu.VMEM((2, GLA_BLOCK, hv), jnp.bfloat16),
            pltpu.SMEM((2,), jnp.float32),
        ],
        compiler_params=_cparams(("arbitrary",)),
        name="gla_core",
    )(qkvr, qkvr, qkvr, z, w_gate, b_gate, qkvr, g_out, w_out, x3d)


def _swa_kernel(q_ref, k_ref, v_ref, o_ref, lse_ref, kprev_scr, vprev_scr, *, nsub):
    blk = SWA_SPAN
    first = pl.program_id(2) == 0

    @pl.when(first)
    def _():
        kprev_scr[...] = jnp.zeros_like(kprev_scr)
        vprev_scr[...] = jnp.zeros_like(vprev_scr)

    qi = lax.broadcasted_iota(jnp.int32, (blk, 2 * blk), 0)
    kc = lax.broadcasted_iota(jnp.int32, (blk, 2 * blk), 1)
    band = (kc >= qi) & (kc <= qi + blk)
    lo = jnp.where(first, blk, 0)
    band_first = (kc >= jnp.maximum(qi, lo)) & (kc <= qi + blk)
    lane = lax.broadcasted_iota(jnp.int32, (blk, LANES), 1)
    for r in range(q_ref.shape[1]):
        for s in range(nsub):
            rows = slice(s * blk, (s + 1) * blk)
            valid = band_first if s == 0 else band
            lse_tile = jnp.zeros((blk, LANES), jnp.float32)
            for h in range(SWA_HEADS):
                hs = slice(h * SWA_HEAD_DIM, (h + 1) * SWA_HEAD_DIM)
                q = q_ref[0, r, rows, hs]
                if s == 0:
                    k = jnp.concatenate([kprev_scr[r, :, hs], k_ref[0, r, rows, hs]], axis=0)
                    v = jnp.concatenate([vprev_scr[r, :, hs], v_ref[0, r, rows, hs]], axis=0)
                else:
                    k = k_ref[0, r, (s - 1) * blk:(s + 1) * blk, hs]
                    v = v_ref[0, r, (s - 1) * blk:(s + 1) * blk, hs]
                sc = jnp.where(valid, _dot_nt(q, k), -jnp.inf)
                m = jnp.max(sc, axis=-1, keepdims=True)
                p = jnp.exp2(sc - m)
                l = jnp.sum(p, axis=-1, keepdims=True)
                o = jnp.dot(p.astype(jnp.bfloat16), v, preferred_element_type=jnp.float32) / l
                o_ref[0, r, rows, hs] = o.astype(o_ref.dtype)
                lse_tile = jnp.where(lane == h, (m + jnp.log2(l)) * LN2, lse_tile)
            lse_ref[0, r, rows, :] = lse_tile
        kprev_scr[r] = k_ref[0, r, (nsub - 1) * blk:nsub * blk, :]
        vprev_scr[r] = v_ref[0, r, (nsub - 1) * blk:nsub * blk, :]


def _swa_group(qkv, *, dil, rows, subseqs):
    batch, _, sub_len, _ = qkv.shape
    hd = SWA_HEADS * SWA_HEAD_DIM
    col = lambda c: (lambda b, r, j: (b, r, j, c))
    return pl.pallas_call(
        functools.partial(_swa_kernel, nsub=rows // SWA_SPAN),
        grid=(batch, dil // subseqs, sub_len // rows),
        in_specs=[pl.BlockSpec((1, subseqs, rows, hd), col(0)),
                  pl.BlockSpec((1, subseqs, rows, hd), col(1)),
                  pl.BlockSpec((1, subseqs, rows, hd), col(2))],
        out_specs=[pl.BlockSpec((1, subseqs, rows, hd), col(0)),
                   pl.BlockSpec((1, subseqs, rows, LANES), col(0))],
        out_shape=[jax.ShapeDtypeStruct((batch, dil, sub_len, hd), jnp.bfloat16),
                   jax.ShapeDtypeStruct((batch, dil, sub_len, LANES), jnp.float32)],
        scratch_shapes=[pltpu.VMEM((subseqs, SWA_SPAN, hd), jnp.bfloat16),
                        pltpu.VMEM((subseqs, SWA_SPAN, hd), jnp.bfloat16)],
        compiler_params=_cparams(("parallel", "parallel", "arbitrary")),
        name=f"swa_attn_d{dil}",
    )(qkv, qkv, qkv)


def _swa_out_kernel(o1_ref, o2_ref, o3_ref, l1_ref, l2_ref, l3_ref, w_ref, x_ref, out_ref):
    tm = x_ref.shape[1]

    def rows_of(ref, dil, sb):
        n = PERM_ROWS // dil
        if dil == 1:
            return ref[0, 0, sb * PERM_ROWS:(sb + 1) * PERM_ROWS, :]
        return jnp.concatenate([ref[0, r, sb * n:(sb + 1) * n, :] for r in range(dil)], axis=0)

    def natural_bf16(ref, dil, sb):
        rm = rows_of(ref, dil, sb)
        if dil == 1:
            return rm.astype(jnp.float32)
        return jnp.dot(_perm_matrix(dil, to_natural=True), rm,
                       preferred_element_type=jnp.float32)

    def natural_f32(ref, dil, sb):
        rm = rows_of(ref, dil, sb)
        if dil == 1:
            return rm
        perm = _perm_matrix(dil, to_natural=True)
        hi = rm.astype(jnp.bfloat16)
        lo = (rm - hi.astype(jnp.float32)).astype(jnp.bfloat16)
        return (jnp.dot(perm, hi, preferred_element_type=jnp.float32)
                + jnp.dot(perm, lo, preferred_element_type=jnp.float32))

    dils = [d for _, d in SWA_PATTERNS]
    merged = []
    for sb in range(tm // PERM_ROWS):
        l1, l2, l3 = (natural_f32(r, d, sb) for r, d in zip((l1_ref, l2_ref, l3_ref), dils))
        o1, o2, o3 = (natural_bf16(r, d, sb) for r, d in zip((o1_ref, o2_ref, o3_ref), dils))
        mx = jnp.maximum(jnp.maximum(l1, l2), l3)
        e1, e2, e3 = jnp.exp(l1 - mx), jnp.exp(l2 - mx), jnp.exp(l3 - mx)
        inv = 1.0 / (e1 + e2 + e3)
        w2, w3 = e2 * inv, e3 * inv
        parts = []
        for h in range(SWA_HEADS):
            hs = slice(h * SWA_HEAD_DIM, (h + 1) * SWA_HEAD_DIM)
            base = o1[:, hs]
            parts.append((base + w2[:, h:h + 1] * (o2[:, hs] - base)
                          + w3[:, h:h + 1] * (o3[:, hs] - base)).astype(jnp.bfloat16))
        merged.append(jnp.concatenate(parts, axis=-1))
    o = jnp.concatenate(merged, axis=0)
    out_ref[0] = x_ref[0] + jnp.dot(o, w_ref[...], preferred_element_type=jnp.float32)


def _swa_out(outs, lses, w, x3d, *, layer, tm):
    batch, seq, d = x3d.shape
    hd = w.shape[1]
    assert tm % PERM_ROWS == 0

    def group_tile(dil, n):
        return pl.BlockSpec((1, dil, tm // dil, n), lambda b, i: (b, 0, i, 0))

    dils = [dil for _, dil in SWA_PATTERNS]
    return pl.pallas_call(
        _swa_out_kernel,
        grid=(batch, seq // tm),
        in_specs=[group_tile(dil, hd) for dil in dils] + [group_tile(dil, LANES) for dil in dils]
        + [pl.BlockSpec((None, hd, d), lambda b, i: (layer, 0, 0)),
           pl.BlockSpec((1, tm, d), lambda b, i: (b, i, 0))],
        out_specs=pl.BlockSpec((1, tm, d), lambda b, i: (b, i, 0)),
        out_shape=jax.ShapeDtypeStruct((batch, seq, d), jnp.float32),
        compiler_params=_cparams(("parallel", "parallel")),
        name="swa_merge_out",
    )(*outs, *lses, w, x3d)


def kernel(x, norm_mix, norm_mlp, gla_w_in, gla_w_gate_up, gla_b_gate, gla_g_out, gla_w_out,
           swa_w_qkv, swa_g_q, swa_g_k, swa_w_out, mlp_w_up, mlp_w_down):
    batch, seq, d = x.shape
    depth = norm_mix.shape[0]
    hk = GLA_HEADS * GLA_DK
    hv = GLA_HEADS * GLA_DV
    n_main = 2 * hk + 2 * hv
    bf = jnp.bfloat16
    assert seq % GLA_BLOCK == 0
    for window, dilation in SWA_PATTERNS:
        assert seq % window == 0

    gla_w_out_bf, swa_w_qkv_bf, swa_w_out_bf = (w.astype(bf) for w in
                                                (gla_w_out, swa_w_qkv, swa_w_out))
    mlp_w_up_bf, mlp_w_down_bf = mlp_w_up.astype(bf), mlp_w_down.astype(bf)

    x2d = x.reshape(batch * seq, d)
    for i in range(depth):
        j = i // 2
        if i % 2 == 0:
            w_in = gla_w_in[j]
            w_z = jnp.pad(w_in[:, n_main:], ((0, 0), (0, LANES - GLA_GATE_RANK))).astype(bf)
            qkvr, z = _gla_in_proj(x2d, norm_mix[i], w_in[:, :n_main].astype(bf), w_z, tm=1024)
            w_gate = jnp.pad(gla_w_gate_up[j], ((0, LANES - GLA_GATE_RANK), (0, 0))).astype(bf)
            x2d = _gla_core(qkvr.reshape(batch, seq, n_main), z.reshape(batch, seq, LANES),
                            w_gate, gla_b_gate[j].reshape(1, hk), gla_g_out[j].reshape(1, hv),
                            gla_w_out_bf, x2d.reshape(batch, seq, d),
                            layer=j).reshape(batch * seq, d)
        else:
            outs, lses = [], []
            for gi, (window, dil) in enumerate(SWA_PATTERNS):
                assert window // dil == SWA_SPAN
                gains = jnp.stack([swa_g_q[j, gi] * (SWA_HEAD_DIM ** -0.5 * LOG2E), swa_g_k[j, gi]])
                qkv = _swa_proj(x2d, norm_mix[i], swa_w_qkv_bf, gains, layer=j, batch=batch,
                                seq=seq, group=gi, dil=dil, tm=1024)
                rows = min(SWA_ATTN_ROWS, seq // dil)
                o, lse = _swa_group(qkv, dil=dil, rows=rows, subseqs=SWA_ATTN_ROWS // rows)
                outs.append(o)
                lses.append(lse)
            x2d = _swa_out(outs, lses, swa_w_out_bf, x2d.reshape(batch, seq, d),
                           layer=j, tm=512).reshape(batch * seq, d)
        x2d = _mlp(x2d, norm_mlp[i], mlp_w_up_bf, mlp_w_down_bf, layer=i, tm=1024)
    return x2d.reshape(batch, seq, d)
```

```python
import functools

import jax
import jax.numpy as jnp
from jax import lax
from jax.experimental import pallas as pl
from jax.experimental.pallas import tpu as pltpu

EPS = 1e-6
LOG2E = 1.4426950408889634
LN2 = 0.6931471805599453

GLA_HEADS = 4
GLA_DK = 128
GLA_DV = 256
GLA_GATE_RANK = 16
GLA_GATE_TAU = 16.0
GLA_CHUNK = 64
GLA_SUB = 16
GLA_BLOCK = 512
GLA_FAST_CHUNK = 256
GLA_FAST_MAX_DECAY = 40.0

SWA_PATTERNS = ((128, 1), (512, 4), (2048, 16))
SWA_HEADS = 8
SWA_HEAD_DIM = 128
SWA_SPAN = 128
PERM_ROWS = 256
SWA_ATTN_ROWS = 512

LANES = 128
VMEM_LIMIT_BYTES = 56 * 1024 * 1024


def _cparams(semantics):
    return pltpu.CompilerParams(dimension_semantics=semantics,
                                vmem_limit_bytes=VMEM_LIMIT_BYTES)


def _rms_scale(x32):
    return lax.rsqrt(jnp.mean(x32 * x32, axis=-1, keepdims=True) + EPS)


def _perm_matrix(dil, to_natural):
    n = PERM_ROWS // dil
    row = lax.broadcasted_iota(jnp.int32, (PERM_ROWS, PERM_ROWS), 0)
    col = lax.broadcasted_iota(jnp.int32, (PERM_ROWS, PERM_ROWS), 1)
    tok, rm = (row, col) if to_natural else (col, row)
    rm_of_tok = (tok & (dil - 1)) * n + (tok >> (dil.bit_length() - 1))
    return jnp.where(rm == rm_of_tok, 1.0, 0.0).astype(jnp.bfloat16)


def _norm_rows(x32, g):
    return (x32 * _rms_scale(x32) * g).astype(jnp.bfloat16)


def _pipelined_tiles(norm_into, project_from):
    t = pl.program_id(0)

    @pl.when(t == 0)
    def _():
        norm_into(0)

    @pl.when((t > 0) & (t % 2 == 1))
    def _():
        norm_into(1)
        project_from(0)

    @pl.when((t > 0) & (t % 2 == 0))
    def _():
        norm_into(0)
        project_from(1)


def _swa_proj_kernel(x_ref, g_ref, w_ref, gqk_ref, o_ref, h_scr, *, dil):
    tm = x_ref.shape[0]
    n = PERM_ROWS // dil
    hd = SWA_HEADS * SWA_HEAD_DIM

    def norm_into(slot):
        h = _norm_rows(x_ref[...], g_ref[...])
        if dil == 1:
            h_scr[slot] = h
        else:
            perm = _perm_matrix(dil, to_natural=False)
            for sb in range(tm // PERM_ROWS):
                rows = slice(sb * PERM_ROWS, (sb + 1) * PERM_ROWS)
                h_scr[slot, rows, :] = jnp.dot(
                    perm, h[rows], preferred_element_type=jnp.float32).astype(jnp.bfloat16)

    def store(y_bf16, cols):
        if dil == 1:
            o_ref[0, 0, :, cols] = y_bf16
        else:
            for sb in range(tm // PERM_ROWS):
                for r in range(dil):
                    src = sb * PERM_ROWS + r * n
                    o_ref[0, r, sb * n:(sb + 1) * n, cols] = y_bf16[src:src + n]

    def project_from(slot):
        for j in range(3):
            y = jnp.dot(h_scr[slot], w_ref[:, j * hd:(j + 1) * hd],
                        preferred_element_type=jnp.float32)
            if j == 2:
                store(y.astype(jnp.bfloat16), slice(j * hd, (j + 1) * hd))
                continue
            gain = gqk_ref[j:j + 1, :]
            for h in range(SWA_HEADS):
                yh = y[:, h * SWA_HEAD_DIM:(h + 1) * SWA_HEAD_DIM]
                store((yh * _rms_scale(yh) * gain).astype(jnp.bfloat16),
                      slice(j * hd + h * SWA_HEAD_DIM, j * hd + (h + 1) * SWA_HEAD_DIM))

    _pipelined_tiles(norm_into, project_from)


def _swa_proj(x2d, g, w_qkv, gains_qk, *, layer, batch, seq, group, dil, tm):
    d = x2d.shape[1]
    gcols = 3 * SWA_HEADS * SWA_HEAD_DIM
    tiles_per_seq = seq // tm
    n_tiles = batch * tiles_per_seq

    def out_map(t):
        tile = jnp.maximum(t - 1, 0)
        return (tile // tiles_per_seq, 0, tile % tiles_per_seq, 0)

    return pl.pallas_call(
        functools.partial(_swa_proj_kernel, dil=dil),
        grid=(n_tiles + 1,),
        in_specs=[
            pl.BlockSpec((tm, d), lambda t: (jnp.minimum(t, n_tiles - 1), 0)),
            pl.BlockSpec((1, d), lambda t: (0, 0)),
            pl.BlockSpec((None, d, gcols), lambda t: (layer, 0, group)),
            pl.BlockSpec((2, SWA_HEAD_DIM), lambda t: (0, 0)),
        ],
        out_specs=pl.BlockSpec((1, dil, tm // dil, gcols), out_map),
        out_shape=jax.ShapeDtypeStruct((batch, dil, seq // dil, gcols), jnp.bfloat16),
        scratch_shapes=[pltpu.VMEM((2, tm, d), jnp.bfloat16)],
        compiler_params=_cparams(("arbitrary",)),
        name=f"swa_proj_d{dil}",
    )(x2d, g.reshape(1, d), w_qkv, gains_qk)


def _gla_in_kernel(x_ref, g_ref, w_ref, wz_ref, o_ref, z_ref, h_scr):
    def norm_into(slot):
        h_scr[slot] = _norm_rows(x_ref[...], g_ref[...])

    def project_from(slot):
        h = h_scr[slot]
        z_ref[...] = jnp.dot(h, wz_ref[...], preferred_element_type=jnp.float32)
        tn = GLA_HEADS * GLA_DV
        for j in range(w_ref.shape[1] // tn):
            cols = slice(j * tn, (j + 1) * tn)
            o_ref[:, cols] = jnp.dot(h, w_ref[:, cols],
                                     preferred_element_type=jnp.float32).astype(o_ref.dtype)

    _pipelined_tiles(norm_into, project_from)


def _gla_in_proj(x2d, g, w_main, w_z, *, layer, tm):
    m, d = x2d.shape
    n = w_main.shape[2]
    nz = w_z.shape[1]
    n_tiles = m // tm
    cur = lambda t: (jnp.maximum(t - 1, 0), 0)
    return pl.pallas_call(
        _gla_in_kernel,
        grid=(n_tiles + 1,),
        in_specs=[
            pl.BlockSpec((tm, d), lambda t: (jnp.minimum(t, n_tiles - 1), 0)),
            pl.BlockSpec((1, d), lambda t: (0, 0)),
            pl.BlockSpec((None, d, n), lambda t: (layer, 0, 0)),
            pl.BlockSpec((d, nz), lambda t: (0, 0)),
        ],
        out_specs=[pl.BlockSpec((tm, n), cur), pl.BlockSpec((tm, nz), cur)],
        out_shape=[
            jax.ShapeDtypeStruct((m, n), jnp.bfloat16),
            jax.ShapeDtypeStruct((m, nz), jnp.float32),
        ],
        scratch_shapes=[pltpu.VMEM((2, tm, d), jnp.bfloat16)],
        compiler_params=_cparams(("arbitrary",)),
        name="gla_in_proj",
    )(x2d, g.reshape(1, d), w_main, w_z)


def _mlp_kernel(x_ref, xn_ref, g_ref, wu_ref, wd_ref, out_ref, h_scr):
    @pl.when(pl.program_id(0) == 0)
    def _():
        h_scr[...] = _norm_rows(x_ref[...], g_ref[...])

    u = jnp.dot(h_scr[...], wu_ref[...], preferred_element_type=jnp.float32)
    a = jnp.square(jnp.maximum(u, 0.0)).astype(jnp.bfloat16)
    h_scr[...] = _norm_rows(xn_ref[...], g_ref[...])
    out_ref[...] = x_ref[...] + jnp.dot(a, wd_ref[...], preferred_element_type=jnp.float32)


def _mlp(x2d, g, w_up, w_down, *, layer, tm):
    m, d = x2d.shape
    ff = w_up.shape[2]
    n_tiles = m // tm
    resident = pl.Buffered(1)
    return pl.pallas_call(
        _mlp_kernel,
        grid=(n_tiles,),
        in_specs=[
            pl.BlockSpec((tm, d), lambda i: (i, 0)),
            pl.BlockSpec((tm, d), lambda i: (jnp.minimum(i + 1, n_tiles - 1), 0)),
            pl.BlockSpec((1, d), lambda i: (0, 0)),
            pl.BlockSpec((None, d, ff), lambda i: (layer, 0, 0), pipeline_mode=resident),
            pl.BlockSpec((None, ff, d), lambda i: (layer, 0, 0), pipeline_mode=resident),
        ],
        out_specs=pl.BlockSpec((tm, d), lambda i: (i, 0)),
        out_shape=jax.ShapeDtypeStruct((m, d), jnp.float32),
        scratch_shapes=[pltpu.VMEM((tm, d), jnp.bfloat16)],
        compiler_params=_cparams(("arbitrary",)),
        name="sqrelu_mlp",
    )(x2d, x2d, g.reshape(1, d), w_up, w_down)


def _dot_nt(a, b):
    return lax.dot_general(a, b, (((1,), (1,)), ((), ())),
                           preferred_element_type=jnp.float32)


def _dot_tn(a, b):
    return lax.dot_general(a, b, (((0,), (0,)), ((), ())),
                           preferred_element_type=jnp.float32)


def _gla_kernel(q_ref, k_ref, v_ref, z_ref, wg_ref, bg_ref, r_ref, go_ref, wo_ref, x_ref,
                out_ref, state_scr, b_scr, k_scr, brel_scr, o_scr, decay_scr, *, blocks_per_seq):
    C, SUB, CF = GLA_CHUNK, GLA_SUB, GLA_FAST_CHUNK
    nsub = C // SUB
    nchunk = GLA_BLOCK // C
    t = pl.program_id(0)

    row = lax.broadcasted_iota(jnp.int32, (CF, CF), 0)
    col = lax.broadcasted_iota(jnp.int32, (CF, CF), 1)
    causal = row >= col
    tri = jnp.where(causal, 1.0, 0.0).astype(jnp.bfloat16)

    def gates(slot):
        zb = z_ref[0].astype(jnp.bfloat16)
        gp = jnp.dot(zb, wg_ref[...], preferred_element_type=jnp.float32) + bg_ref[...]
        g = (jnp.minimum(gp, 0.0) - jnp.log(1.0 + jnp.exp(-jnp.abs(gp)))) * (1.0 / GLA_GATE_TAU)
        g_hi = g.astype(jnp.bfloat16)
        g_lo = (g - g_hi.astype(jnp.float32)).astype(jnp.bfloat16)
        decay_max = jnp.zeros((1, GLA_DK), jnp.float32)
        for h in range(GLA_HEADS):
            ks = slice(h * GLA_DK, (h + 1) * GLA_DK)
            for cf in range(GLA_BLOCK // CF):
                rows = slice(cf * CF, (cf + 1) * CF)
                g_hl = jnp.concatenate([g_hi[rows, ks], g_lo[rows, ks]], axis=-1)
                b_hl = jnp.dot(tri, g_hl, preferred_element_type=jnp.float32)
                b = b_hl[:, :GLA_DK] + b_hl[:, GLA_DK:]
                b_scr[slot, h, rows, :] = b
                decay_max = jnp.maximum(decay_max, -b[CF - 1:CF, :])
        decay_scr[slot] = jnp.max(decay_max)

    def chunk_update(slot, ci_rows, h, st, a, q_in, k, v, b):
        vs = slice(h * GLA_DV, (h + 1) * GLA_DV)
        b_last = b[b.shape[0] - 1:, :]
        o = (jnp.dot(a, v, preferred_element_type=jnp.float32)
             + _dot_nt(q_in, st.astype(jnp.bfloat16)))
        o_scr[slot, ci_rows, vs] = o.astype(o_scr.dtype)
        k_dec = (k * jnp.exp(b_last - b)).astype(jnp.bfloat16)
        return st * jnp.exp(b_last) + _dot_tn(v, k_dec)

    def fast_path(slot, keep):
        for cf in range(GLA_BLOCK // CF):
            rows = slice(cf * CF, (cf + 1) * CF)
            for h in range(GLA_HEADS):
                ks = slice(h * GLA_DK, (h + 1) * GLA_DK)
                vs = slice(h * GLA_DV, (h + 1) * GLA_DV)
                b = b_scr[slot, h, rows, :]
                q = q_ref[0, rows, ks].astype(jnp.float32) * (GLA_DK ** -0.5)
                k = k_ref[0, rows, ks].astype(jnp.float32)
                v = v_ref[0, rows, vs]
                q_in = (q * jnp.exp(b)).astype(jnp.bfloat16)
                k_out = (k * jnp.exp(-b)).astype(jnp.bfloat16)
                a = jnp.where(causal, _dot_nt(q_in, k_out), 0.0).astype(jnp.bfloat16)
                st = state_scr[h] * keep if cf == 0 else state_scr[h]
                state_scr[h] = chunk_update(slot, rows, h, st, a, q_in, k, v, b)

    def robust_path(slot, keep):
        srow = lax.broadcasted_iota(jnp.int32, (SUB, C), 0)
        scol = lax.broadcasted_iota(jnp.int32, (SUB, C), 1)

        def chunk_body(ci, carry):
            rows = pl.ds(pl.multiple_of(ci * C, C), C)
            keep_c = jnp.where(ci == 0, keep, 1.0)
            for h in range(GLA_HEADS):
                ks = slice(h * GLA_DK, (h + 1) * GLA_DK)
                vs = slice(h * GLA_DV, (h + 1) * GLA_DV)
                c0 = ci * C
                before = b_scr[slot, h, pl.ds(jnp.maximum(c0 - 1, 0), 1), :]
                base = jnp.where(c0 % CF == 0, 0.0, before)
                b = b_scr[slot, h, rows, :] - base
                q = q_ref[0, rows, ks].astype(jnp.float32) * (GLA_DK ** -0.5)
                k = k_ref[0, rows, ks].astype(jnp.float32)
                v = v_ref[0, rows, vs]
                k_scr[h] = k
                brel_scr[h] = b

                refs = [jnp.zeros((1, GLA_DK), jnp.float32)]
                for i in range(1, nsub):
                    refs.append(brel_scr[h, i * SUB - 1:i * SUB, :])
                ref_rows = jnp.concatenate(
                    [jnp.broadcast_to(r, (SUB, GLA_DK)) for r in refs], axis=0)
                qe = (q * jnp.exp(b - ref_rows)).astype(jnp.bfloat16)

                a_rows = []
                for i in range(nsub):
                    r0 = i * SUB
                    qi = q[r0:r0 + SUB]
                    bi = b[r0:r0 + SUB]
                    key = jnp.where(srow >= scol - r0, scol, -1)
                    blk = jnp.zeros((SUB, C), jnp.float32)
                    for jj in range(SUB):
                        j = r0 + jj
                        bj = brel_scr[h, j:j + 1, :]
                        kj = k_scr[h, j:j + 1, :]
                        e = jnp.exp(jnp.minimum(bi - bj, 0.0))
                        s = jnp.sum(qi * e * kj, axis=-1, keepdims=True)
                        blk = jnp.where(key == j, s, blk)
                    if i > 0:
                        ke = (k * jnp.exp(jnp.minimum(refs[i] - b, 0.0))).astype(jnp.bfloat16)
                        off = _dot_nt(qe[r0:r0 + SUB], ke)
                        blk = jnp.where(scol < r0, off, blk)
                    a_rows.append(blk)
                a = jnp.concatenate(a_rows, axis=0).astype(jnp.bfloat16)
                q_in = (q * jnp.exp(b)).astype(jnp.bfloat16)
                state_scr[h] = chunk_update(slot, rows, h, state_scr[h] * keep_c,
                                            a, q_in, k, v, b)
            return carry

        lax.fori_loop(0, nchunk, chunk_body, 0)

    def output_stage(slot):
        parts = []
        for h in range(GLA_HEADS):
            vs = slice(h * GLA_DV, (h + 1) * GLA_DV)
            o = o_scr[slot, :, vs].astype(jnp.float32)
            r = r_ref[0, :, vs].astype(jnp.float32)
            y = o * _rms_scale(o) * go_ref[:, vs]
            parts.append((y * (r / (1.0 + jnp.exp(-r)))).astype(jnp.bfloat16))
        y = jnp.concatenate(parts, axis=-1)
        out_ref[0] = x_ref[0] + jnp.dot(y, wo_ref[...], preferred_element_type=jnp.float32)

    def step(cur, nxt):
        keep = jnp.where((t - 1) % blocks_per_seq == 0, 0.0, 1.0)
        mild = decay_scr[cur] <= GLA_FAST_MAX_DECAY

        @pl.when(mild)
        def _():
            gates(nxt)
            fast_path(cur, keep)
            output_stage(nxt)

        @pl.when(jnp.logical_not(mild))
        def _():
            gates(nxt)
            robust_path(cur, keep)
            output_stage(nxt)

    @pl.when(t == 0)
    def _():
        state_scr[...] = jnp.zeros_like(state_scr)
        o_scr[...] = jnp.zeros_like(o_scr)
        gates(0)

    @pl.when((t > 0) & (t % 2 == 1))
    def _():
        step(0, 1)

    @pl.when((t > 0) & (t % 2 == 0))
    def _():
        step(1, 0)


def _gla_core(qkvr, z, w_gate, b_gate, g_out, w_out, x3d, *, layer):
    batch, seq, d = x3d.shape
    hk = GLA_HEADS * GLA_DK
    hv = GLA_HEADS * GLA_DV
    tb = GLA_BLOCK
    bps = seq // tb
    n_blocks = batch * bps

    def lagged(lag, col):
        def index_map(t):
            blk = jnp.clip(t - lag, 0, n_blocks - 1)
            return (blk // bps, blk % bps, col)
        return index_map

    return pl.pallas_call(
        functools.partial(_gla_kernel, blocks_per_seq=bps),
        grid=(n_blocks + 2,),
        in_specs=[
            pl.BlockSpec((1, tb, hk), lagged(1, 0)),
            pl.BlockSpec((1, tb, hk), lagged(1, 1)),
            pl.BlockSpec((1, tb, hv), lagged(1, 1)),
            pl.BlockSpec((1, tb, LANES), lagged(0, 0)),
            pl.BlockSpec((LANES, hk), lambda t: (0, 0)),
            pl.BlockSpec((1, hk), lambda t: (0, 0)),
            pl.BlockSpec((1, tb, hv), lagged(2, 2)),
            pl.BlockSpec((1, hv), lambda t: (0, 0)),
            pl.BlockSpec((None, hv, d), lambda t: (layer, 0, 0)),
            pl.BlockSpec((1, tb, d), lagged(2, 0)),
        ],
        out_specs=pl.BlockSpec((1, tb, d), lagged(2, 0)),
        out_shape=jax.ShapeDtypeStruct((batch, seq, d), jnp.float32),
        scratch_shapes=[
            pltpu.VMEM((GLA_HEADS, GLA_DV, GLA_DK), jnp.float32),
            pltpu.VMEM((2, GLA_HEADS, GLA_BLOCK, GLA_DK), jnp.float32),
            pltpu.VMEM((GLA_HEADS, GLA_CHUNK, GLA_DK), jnp.float32),
            pltpu.VMEM((GLA_HEADS, GLA_CHUNK, GLA_DK), jnp.float32),
            pltpu.VMEM((2, GLA_BLOCK, hv), jnp.bfloat16),
            pltpu.SMEM((2,), jnp.float32),
        ],
        compiler_params=_cparams(("arbitrary",)),
        name="gla_core",
    )(qkvr, qkvr, qkvr, z, w_gate, b_gate, qkvr, g_out, w_out, x3d)


def _swa_kernel(q_ref, k_ref, v_ref, o_ref, st_ref, kprev_scr, vprev_scr, *, nsub):
    blk = SWA_SPAN
    first = pl.program_id(2) == 0

    @pl.when(first)
    def _():
        kprev_scr[...] = jnp.zeros_like(kprev_scr)
        vprev_scr[...] = jnp.zeros_like(vprev_scr)

    qi = lax.broadcasted_iota(jnp.int32, (blk, 2 * blk), 0)
    kc = lax.broadcasted_iota(jnp.int32, (blk, 2 * blk), 1)
    band = (kc >= qi) & (kc <= qi + blk)
    lo = jnp.where(first, blk, 0)
    band_first = (kc >= jnp.maximum(qi, lo)) & (kc <= qi + blk)
    lane = lax.broadcasted_iota(jnp.int32, (blk, LANES), 1)
    for r in range(q_ref.shape[1]):
        for s in range(nsub):
            rows = slice(s * blk, (s + 1) * blk)
            valid = band_first if s == 0 else band
            m_tile = jnp.zeros((blk, LANES), jnp.float32)
            l_tile = jnp.ones((blk, LANES), jnp.float32)
            for h in range(SWA_HEADS):
                hs = slice(h * SWA_HEAD_DIM, (h + 1) * SWA_HEAD_DIM)
                q = q_ref[0, r, rows, hs]
                if s == 0:
                    k = jnp.concatenate([kprev_scr[r, :, hs], k_ref[0, r, rows, hs]], axis=0)
                    v = jnp.concatenate([vprev_scr[r, :, hs], v_ref[0, r, rows, hs]], axis=0)
                else:
                    k = k_ref[0, r, (s - 1) * blk:(s + 1) * blk, hs]
                    v = v_ref[0, r, (s - 1) * blk:(s + 1) * blk, hs]
                sc = jnp.where(valid, _dot_nt(q, k), -jnp.inf)
                m = jnp.max(sc, axis=-1, keepdims=True)
                p = jnp.exp2(sc - m)
                l = jnp.sum(p, axis=-1, keepdims=True)
                o = jnp.dot(p.astype(jnp.bfloat16), v, preferred_element_type=jnp.float32)
                o_ref[0, r, rows, hs] = o.astype(o_ref.dtype)
                m_tile = jnp.where(lane == h, m, m_tile)
                l_tile = jnp.where(lane == h, l, l_tile)
            st_ref[0, r, rows, :LANES] = m_tile
            st_ref[0, r, rows, LANES:] = l_tile
        kprev_scr[r] = k_ref[0, r, (nsub - 1) * blk:nsub * blk, :]
        vprev_scr[r] = v_ref[0, r, (nsub - 1) * blk:nsub * blk, :]


def _swa_group(qkv, *, dil, rows, subseqs):
    batch, _, sub_len, _ = qkv.shape
    hd = SWA_HEADS * SWA_HEAD_DIM
    col = lambda c: (lambda b, r, j: (b, r, j, c))
    return pl.pallas_call(
        functools.partial(_swa_kernel, nsub=rows // SWA_SPAN),
        grid=(batch, dil // subseqs, sub_len // rows),
        in_specs=[pl.BlockSpec((1, subseqs, rows, hd), col(0)),
                  pl.BlockSpec((1, subseqs, rows, hd), col(1)),
                  pl.BlockSpec((1, subseqs, rows, hd), col(2))],
        out_specs=[pl.BlockSpec((1, subseqs, rows, hd), col(0)),
                   pl.BlockSpec((1, subseqs, rows, 2 * LANES), col(0))],
        out_shape=[jax.ShapeDtypeStruct((batch, dil, sub_len, hd), jnp.bfloat16),
                   jax.ShapeDtypeStruct((batch, dil, sub_len, 2 * LANES), jnp.float32)],
        scratch_shapes=[pltpu.VMEM((subseqs, SWA_SPAN, hd), jnp.bfloat16),
                        pltpu.VMEM((subseqs, SWA_SPAN, hd), jnp.bfloat16)],
        compiler_params=_cparams(("parallel", "parallel", "arbitrary")),
        name=f"swa_attn_d{dil}",
    )(qkv, qkv, qkv)


def _swa_out_kernel(o1_ref, o2_ref, o3_ref, l1_ref, l2_ref, l3_ref, w_ref, x_ref, out_ref):
    tm = x_ref.shape[1]

    def rows_of(ref, dil, sb):
        n = PERM_ROWS // dil
        if dil == 1:
            return ref[0, 0, sb * PERM_ROWS:(sb + 1) * PERM_ROWS, :]
        return jnp.concatenate([ref[0, r, sb * n:(sb + 1) * n, :] for r in range(dil)], axis=0)

    def natural_bf16(ref, dil, sb):
        rm = rows_of(ref, dil, sb)
        if dil == 1:
            return rm.astype(jnp.float32)
        return jnp.dot(_perm_matrix(dil, to_natural=True), rm,
                       preferred_element_type=jnp.float32)

    def natural_f32(ref, dil, sb):
        rm = rows_of(ref, dil, sb)
        if dil == 1:
            return rm
        perm = _perm_matrix(dil, to_natural=True)
        hi = rm.astype(jnp.bfloat16)
        lo = (rm - hi.astype(jnp.float32)).astype(jnp.bfloat16)
        return (jnp.dot(perm, hi, preferred_element_type=jnp.float32)
                + jnp.dot(perm, lo, preferred_element_type=jnp.float32))

    dils = [d for _, d in SWA_PATTERNS]
    merged = []
    for sb in range(tm // PERM_ROWS):
        stats = [natural_f32(r, d, sb) for r, d in zip((l1_ref, l2_ref, l3_ref), dils)]
        o1, o2, o3 = (natural_bf16(r, d, sb) for r, d in zip((o1_ref, o2_ref, o3_ref), dils))
        sums = [st[:, LANES:] for st in stats]
        lse = [st[:, :LANES] + jnp.log2(l) for st, l in zip(stats, sums)]
        mx = jnp.maximum(jnp.maximum(lse[0], lse[1]), lse[2])
        e = [jnp.exp2(v - mx) for v in lse]
        inv = 1.0 / (e[0] + e[1] + e[2])
        c1, c2, c3 = (eg * inv / l for eg, l in zip(e, sums))
        parts = []
        for h in range(SWA_HEADS):
            hs = slice(h * SWA_HEAD_DIM, (h + 1) * SWA_HEAD_DIM)
            parts.append((c1[:, h:h + 1] * o1[:, hs] + c2[:, h:h + 1] * o2[:, hs]
                          + c3[:, h:h + 1] * o3[:, hs]).astype(jnp.bfloat16))
        merged.append(jnp.concatenate(parts, axis=-1))
    o = jnp.concatenate(merged, axis=0)
    out_ref[0] = x_ref[0] + jnp.dot(o, w_ref[...], preferred_element_type=jnp.float32)


def _swa_out(outs, lses, w, x3d, *, layer, tm):
    batch, seq, d = x3d.shape
    hd = w.shape[1]
    assert tm % PERM_ROWS == 0

    def group_tile(dil, n):
        return pl.BlockSpec((1, dil, tm // dil, n), lambda b, i: (b, 0, i, 0))

    dils = [dil for _, dil in SWA_PATTERNS]
    return pl.pallas_call(
        _swa_out_kernel,
        grid=(batch, seq // tm),
        in_specs=[group_tile(dil, hd) for dil in dils] + [group_tile(dil, 2 * LANES) for dil in dils]
        + [pl.BlockSpec((None, hd, d), lambda b, i: (layer, 0, 0)),
           pl.BlockSpec((1, tm, d), lambda b, i: (b, i, 0))],
        out_specs=pl.BlockSpec((1, tm, d), lambda b, i: (b, i, 0)),
        out_shape=jax.ShapeDtypeStruct((batch, seq, d), jnp.float32),
        compiler_params=_cparams(("parallel", "parallel")),
        name="swa_merge_out",
    )(*outs, *lses, w, x3d)


def kernel(x, norm_mix, norm_mlp, gla_w_in, gla_w_gate_up, gla_b_gate, gla_g_out, gla_w_out,
           swa_w_qkv, swa_g_q, swa_g_k, swa_w_out, mlp_w_up, mlp_w_down):
    batch, seq, d = x.shape
    depth = norm_mix.shape[0]
    hk = GLA_HEADS * GLA_DK
    hv = GLA_HEADS * GLA_DV
    n_main = 2 * hk + 2 * hv
    bf = jnp.bfloat16
    assert seq % GLA_BLOCK == 0
    for window, dilation in SWA_PATTERNS:
        assert seq % window == 0

    gla_w_out_bf, swa_w_qkv_bf, swa_w_out_bf = (w.astype(bf) for w in
                                                (gla_w_out, swa_w_qkv, swa_w_out))
    mlp_w_up_bf, mlp_w_down_bf = mlp_w_up.astype(bf), mlp_w_down.astype(bf)
    gla_w_main_bf = gla_w_in[:, :, :n_main].astype(bf)

    x2d = x.reshape(batch * seq, d)
    for i in range(depth):
        j = i // 2
        if i % 2 == 0:
            w_z = jnp.pad(gla_w_in[j, :, n_main:],
                          ((0, 0), (0, LANES - GLA_GATE_RANK))).astype(bf)
            qkvr, z = _gla_in_proj(x2d, norm_mix[i], gla_w_main_bf, w_z, layer=j, tm=1024)
            w_gate = jnp.pad(gla_w_gate_up[j], ((0, LANES - GLA_GATE_RANK), (0, 0))).astype(bf)
            x2d = _gla_core(qkvr.reshape(batch, seq, n_main), z.reshape(batch, seq, LANES),
                            w_gate, gla_b_gate[j].reshape(1, hk), gla_g_out[j].reshape(1, hv),
                            gla_w_out_bf, x2d.reshape(batch, seq, d),
                            layer=j).reshape(batch * seq, d)
        else:
            outs, lses = [], []
            for gi, (window, dil) in enumerate(SWA_PATTERNS):
                assert window // dil == SWA_SPAN
                gains = jnp.stack([swa_g_q[j, gi] * (SWA_HEAD_DIM ** -0.5 * LOG2E), swa_g_k[j, gi]])
                qkv = _swa_proj(x2d, norm_mix[i], swa_w_qkv_bf, gains, layer=j, batch=batch,
                                seq=seq, group=gi, dil=dil, tm=1024)
                rows = min(SWA_ATTN_ROWS, seq // dil)
                o, lse = _swa_group(qkv, dil=dil, rows=rows, subseqs=SWA_ATTN_ROWS // rows)
                outs.append(o)
                lses.append(lse)
            x2d = _swa_out(outs, lses, swa_w_out_bf, x2d.reshape(batch, seq, d),
                           layer=j, tm=512).reshape(batch * seq, d)
        x2d = _mlp(x2d, norm_mlp[i], mlp_w_up_bf, mlp_w_down_bf, layer=i, tm=1024)
    return x2d.reshape(batch, seq, d)
```

```python
import functools

import jax
import jax.numpy as jnp
from jax import lax
from jax.experimental import pallas as pl
from jax.experimental.pallas import tpu as pltpu

EPS = 1e-6
LOG2E = 1.4426950408889634
LN2 = 0.6931471805599453

GLA_HEADS = 4
GLA_DK = 128
GLA_DV = 256
GLA_GATE_RANK = 16
GLA_GATE_TAU = 16.0
GLA_CHUNK = 64
GLA_SUB = 16
GLA_BLOCK = 512
GLA_FAST_CHUNK = 256
GLA_FAST_MAX_DECAY = 40.0

SWA_PATTERNS = ((128, 1), (512, 4), (2048, 16))
SWA_HEADS = 8
SWA_HEAD_DIM = 128
SWA_SPAN = 128
PERM_ROWS = 256
SWA_ATTN_ROWS = 1024

LANES = 128
VMEM_LIMIT_BYTES = 56 * 1024 * 1024


def _cparams(semantics):
    return pltpu.CompilerParams(dimension_semantics=semantics,
                                vmem_limit_bytes=VMEM_LIMIT_BYTES)


def _rms_scale(x32):
    return lax.rsqrt(jnp.mean(x32 * x32, axis=-1, keepdims=True) + EPS)


def _perm_matrix(dil, to_natural):
    n = PERM_ROWS // dil
    row = lax.broadcasted_iota(jnp.int32, (PERM_ROWS, PERM_ROWS), 0)
    col = lax.broadcasted_iota(jnp.int32, (PERM_ROWS, PERM_ROWS), 1)
    tok, rm = (row, col) if to_natural else (col, row)
    rm_of_tok = (tok & (dil - 1)) * n + (tok >> (dil.bit_length() - 1))
    return jnp.where(rm == rm_of_tok, 1.0, 0.0).astype(jnp.bfloat16)


def _norm_rows(x32, g):
    return (x32 * _rms_scale(x32) * g).astype(jnp.bfloat16)


def _pipelined_tiles(norm_into, project_from):
    t = pl.program_id(0)

    @pl.when(t == 0)
    def _():
        norm_into(0)

    @pl.when((t > 0) & (t % 2 == 1))
    def _():
        norm_into(1)
        project_from(0)

    @pl.when((t > 0) & (t % 2 == 0))
    def _():
        norm_into(0)
        project_from(1)


def _swa_proj_kernel(x_ref, g_ref, w_ref, gqk_ref, o_ref, h_scr, *, dil):
    tm = x_ref.shape[0]
    n = PERM_ROWS // dil
    hd = SWA_HEADS * SWA_HEAD_DIM

    def norm_into(slot):
        h = _norm_rows(x_ref[...], g_ref[...])
        if dil == 1:
            h_scr[slot] = h
        else:
            perm = _perm_matrix(dil, to_natural=False)
            for sb in range(tm // PERM_ROWS):
                rows = slice(sb * PERM_ROWS, (sb + 1) * PERM_ROWS)
                h_scr[slot, rows, :] = jnp.dot(
                    perm, h[rows], preferred_element_type=jnp.float32).astype(jnp.bfloat16)

    def store(y_bf16, cols):
        if dil == 1:
            o_ref[0, 0, :, cols] = y_bf16
        else:
            for sb in range(tm // PERM_ROWS):
                for r in range(dil):
                    src = sb * PERM_ROWS + r * n
                    o_ref[0, r, sb * n:(sb + 1) * n, cols] = y_bf16[src:src + n]

    def project_from(slot):
        for j in range(3):
            y = jnp.dot(h_scr[slot], w_ref[:, j * hd:(j + 1) * hd],
                        preferred_element_type=jnp.float32)
            if j == 2:
                store(y.astype(jnp.bfloat16), slice(j * hd, (j + 1) * hd))
                continue
            gain = gqk_ref[j:j + 1, :]
            for h in range(SWA_HEADS):
                yh = y[:, h * SWA_HEAD_DIM:(h + 1) * SWA_HEAD_DIM]
                store((yh * _rms_scale(yh) * gain).astype(jnp.bfloat16),
                      slice(j * hd + h * SWA_HEAD_DIM, j * hd + (h + 1) * SWA_HEAD_DIM))

    _pipelined_tiles(norm_into, project_from)


def _swa_proj(x2d, g, w_qkv, gains_qk, *, layer, batch, seq, group, dil, tm):
    d = x2d.shape[1]
    gcols = 3 * SWA_HEADS * SWA_HEAD_DIM
    tiles_per_seq = seq // tm
    n_tiles = batch * tiles_per_seq

    def out_map(t):
        tile = jnp.maximum(t - 1, 0)
        return (tile // tiles_per_seq, 0, tile % tiles_per_seq, 0)

    return pl.pallas_call(
        functools.partial(_swa_proj_kernel, dil=dil),
        grid=(n_tiles + 1,),
        in_specs=[
            pl.BlockSpec((tm, d), lambda t: (jnp.minimum(t, n_tiles - 1), 0)),
            pl.BlockSpec((1, d), lambda t: (0, 0)),
            pl.BlockSpec((None, d, gcols), lambda t: (layer, 0, group)),
            pl.BlockSpec((2, SWA_HEAD_DIM), lambda t: (0, 0)),
        ],
        out_specs=pl.BlockSpec((1, dil, tm // dil, gcols), out_map),
        out_shape=jax.ShapeDtypeStruct((batch, dil, seq // dil, gcols), jnp.bfloat16),
        scratch_shapes=[pltpu.VMEM((2, tm, d), jnp.bfloat16)],
        compiler_params=_cparams(("arbitrary",)),
        name=f"swa_proj_d{dil}",
    )(x2d, g.reshape(1, d), w_qkv, gains_qk)


def _gla_in_kernel(x_ref, g_ref, w_ref, wz_ref, o_ref, z_ref, h_scr):
    def norm_into(slot):
        h_scr[slot] = _norm_rows(x_ref[...], g_ref[...])

    def project_from(slot):
        h = h_scr[slot]
        z_ref[...] = jnp.dot(h, wz_ref[...], preferred_element_type=jnp.float32)
        tn = GLA_HEADS * GLA_DV
        for j in range(w_ref.shape[1] // tn):
            cols = slice(j * tn, (j + 1) * tn)
            o_ref[:, cols] = jnp.dot(h, w_ref[:, cols],
                                     preferred_element_type=jnp.float32).astype(o_ref.dtype)

    _pipelined_tiles(norm_into, project_from)


def _gla_in_proj(x2d, g, w_main, w_z, *, layer, tm):
    m, d = x2d.shape
    n = w_main.shape[2]
    nz = w_z.shape[1]
    n_tiles = m // tm
    cur = lambda t: (jnp.maximum(t - 1, 0), 0)
    return pl.pallas_call(
        _gla_in_kernel,
        grid=(n_tiles + 1,),
        in_specs=[
            pl.BlockSpec((tm, d), lambda t: (jnp.minimum(t, n_tiles - 1), 0)),
            pl.BlockSpec((1, d), lambda t: (0, 0)),
            pl.BlockSpec((None, d, n), lambda t: (layer, 0, 0)),
            pl.BlockSpec((d, nz), lambda t: (0, 0)),
        ],
        out_specs=[pl.BlockSpec((tm, n), cur), pl.BlockSpec((tm, nz), cur)],
        out_shape=[
            jax.ShapeDtypeStruct((m, n), jnp.bfloat16),
            jax.ShapeDtypeStruct((m, nz), jnp.float32),
        ],
        scratch_shapes=[pltpu.VMEM((2, tm, d), jnp.bfloat16)],
        compiler_params=_cparams(("arbitrary",)),
        name="gla_in_proj",
    )(x2d, g.reshape(1, d), w_main, w_z)


def _mlp_kernel(x_ref, xn_ref, g_ref, wu_ref, wd_ref, out_ref, h_scr):
    @pl.when(pl.program_id(0) == 0)
    def _():
        h_scr[...] = _norm_rows(x_ref[...], g_ref[...])

    u = jnp.dot(h_scr[...], wu_ref[...], preferred_element_type=jnp.float32)
    a = jnp.square(jnp.maximum(u, 0.0)).astype(jnp.bfloat16)
    h_scr[...] = _norm_rows(xn_ref[...], g_ref[...])
    out_ref[...] = x_ref[...] + jnp.dot(a, wd_ref[...], preferred_element_type=jnp.float32)


def _mlp(x2d, g, w_up, w_down, *, layer, tm):
    m, d = x2d.shape
    ff = w_up.shape[2]
    n_tiles = m // tm
    resident = pl.Buffered(1)
    return pl.pallas_call(
        _mlp_kernel,
        grid=(n_tiles,),
        in_specs=[
            pl.BlockSpec((tm, d), lambda i: (i, 0)),
            pl.BlockSpec((tm, d), lambda i: (jnp.minimum(i + 1, n_tiles - 1), 0)),
            pl.BlockSpec((1, d), lambda i: (0, 0)),
            pl.BlockSpec((None, d, ff), lambda i: (layer, 0, 0), pipeline_mode=resident),
            pl.BlockSpec((None, ff, d), lambda i: (layer, 0, 0), pipeline_mode=resident),
        ],
        out_specs=pl.BlockSpec((tm, d), lambda i: (i, 0)),
        out_shape=jax.ShapeDtypeStruct((m, d), jnp.float32),
        scratch_shapes=[pltpu.VMEM((tm, d), jnp.bfloat16)],
        compiler_params=_cparams(("arbitrary",)),
        name="sqrelu_mlp",
    )(x2d, x2d, g.reshape(1, d), w_up, w_down)


def _dot_nt(a, b):
    return lax.dot_general(a, b, (((1,), (1,)), ((), ())),
                           preferred_element_type=jnp.float32)


def _dot_tn(a, b):
    return lax.dot_general(a, b, (((0,), (0,)), ((), ())),
                           preferred_element_type=jnp.float32)


def _gla_kernel(q_ref, k_ref, v_ref, z_ref, wg_ref, bg_ref, r_ref, go_ref, wo_ref, x_ref,
                out_ref, state_scr, b_scr, k_scr, brel_scr, o_scr, decay_scr, *, blocks_per_seq):
    C, SUB, CF = GLA_CHUNK, GLA_SUB, GLA_FAST_CHUNK
    nsub = C // SUB
    nchunk = GLA_BLOCK // C
    t = pl.program_id(0)

    row = lax.broadcasted_iota(jnp.int32, (CF, CF), 0)
    col = lax.broadcasted_iota(jnp.int32, (CF, CF), 1)
    causal = row >= col
    tri = jnp.where(causal, 1.0, 0.0).astype(jnp.bfloat16)

    def gates(slot):
        zb = z_ref[0].astype(jnp.bfloat16)
        gp = jnp.dot(zb, wg_ref[...], preferred_element_type=jnp.float32) + bg_ref[...]
        g = (jnp.minimum(gp, 0.0) - jnp.log(1.0 + jnp.exp(-jnp.abs(gp)))) * (1.0 / GLA_GATE_TAU)
        g_hi = g.astype(jnp.bfloat16)
        g_lo = (g - g_hi.astype(jnp.float32)).astype(jnp.bfloat16)
        decay_max = jnp.zeros((1, GLA_DK), jnp.float32)
        for h in range(GLA_HEADS):
            ks = slice(h * GLA_DK, (h + 1) * GLA_DK)
            for cf in range(GLA_BLOCK // CF):
                rows = slice(cf * CF, (cf + 1) * CF)
                g_hl = jnp.concatenate([g_hi[rows, ks], g_lo[rows, ks]], axis=-1)
                b_hl = jnp.dot(tri, g_hl, preferred_element_type=jnp.float32)
                b = b_hl[:, :GLA_DK] + b_hl[:, GLA_DK:]
                b_scr[slot, h, rows, :] = b
                decay_max = jnp.maximum(decay_max, -b[CF - 1:CF, :])
        decay_scr[slot] = jnp.max(decay_max)

    def chunk_update(slot, ci_rows, h, st, a, q_in, k, v, b):
        vs = slice(h * GLA_DV, (h + 1) * GLA_DV)
        b_last = b[b.shape[0] - 1:, :]
        o = (jnp.dot(a, v, preferred_element_type=jnp.float32)
             + _dot_nt(q_in, st.astype(jnp.bfloat16)))
        o_scr[slot, ci_rows, vs] = o.astype(o_scr.dtype)
        k_dec = (k * jnp.exp(b_last - b)).astype(jnp.bfloat16)
        return st * jnp.exp(b_last) + _dot_tn(v, k_dec)

    def fast_path(slot, keep):
        for cf in range(GLA_BLOCK // CF):
            rows = slice(cf * CF, (cf + 1) * CF)
            for h in range(GLA_HEADS):
                ks = slice(h * GLA_DK, (h + 1) * GLA_DK)
                vs = slice(h * GLA_DV, (h + 1) * GLA_DV)
                b = b_scr[slot, h, rows, :]
                q = q_ref[0, rows, ks].astype(jnp.float32) * (GLA_DK ** -0.5)
                k = k_ref[0, rows, ks].astype(jnp.float32)
                v = v_ref[0, rows, vs]
                q_in = (q * jnp.exp(b)).astype(jnp.bfloat16)
                k_out = (k * jnp.exp(-b)).astype(jnp.bfloat16)
                a = jnp.where(causal, _dot_nt(q_in, k_out), 0.0).astype(jnp.bfloat16)
                st = state_scr[h] * keep if cf == 0 else state_scr[h]
                state_scr[h] = chunk_update(slot, rows, h, st, a, q_in, k, v, b)

    def robust_path(slot, keep):
        srow = lax.broadcasted_iota(jnp.int32, (SUB, C), 0)
        scol = lax.broadcasted_iota(jnp.int32, (SUB, C), 1)

        def chunk_body(ci, carry):
            rows = pl.ds(pl.multiple_of(ci * C, C), C)
            keep_c = jnp.where(ci == 0, keep, 1.0)
            for h in range(GLA_HEADS):
                ks = slice(h * GLA_DK, (h + 1) * GLA_DK)
                vs = slice(h * GLA_DV, (h + 1) * GLA_DV)
                c0 = ci * C
                before = b_scr[slot, h, pl.ds(jnp.maximum(c0 - 1, 0), 1), :]
                base = jnp.where(c0 % CF == 0, 0.0, before)
                b = b_scr[slot, h, rows, :] - base
                q = q_ref[0, rows, ks].astype(jnp.float32) * (GLA_DK ** -0.5)
                k = k_ref[0, rows, ks].astype(jnp.float32)
                v = v_ref[0, rows, vs]
                k_scr[h] = k
                brel_scr[h] = b

                refs = [jnp.zeros((1, GLA_DK), jnp.float32)]
                for i in range(1, nsub):
                    refs.append(brel_scr[h, i * SUB - 1:i * SUB, :])
                ref_rows = jnp.concatenate(
                    [jnp.broadcast_to(r, (SUB, GLA_DK)) for r in refs], axis=0)
                qe = (q * jnp.exp(b - ref_rows)).astype(jnp.bfloat16)

                a_rows = []
                for i in range(nsub):
                    r0 = i * SUB
                    qi = q[r0:r0 + SUB]
                    bi = b[r0:r0 + SUB]
                    key = jnp.where(srow >= scol - r0, scol, -1)
                    blk = jnp.zeros((SUB, C), jnp.float32)
                    for jj in range(SUB):
                        j = r0 + jj
                        bj = brel_scr[h, j:j + 1, :]
                        kj = k_scr[h, j:j + 1, :]
                        e = jnp.exp(jnp.minimum(bi - bj, 0.0))
                        s = jnp.sum(qi * e * kj, axis=-1, keepdims=True)
                        blk = jnp.where(key == j, s, blk)
                    if i > 0:
                        ke = (k * jnp.exp(jnp.minimum(refs[i] - b, 0.0))).astype(jnp.bfloat16)
                        off = _dot_nt(qe[r0:r0 + SUB], ke)
                        blk = jnp.where(scol < r0, off, blk)
                    a_rows.append(blk)
                a = jnp.concatenate(a_rows, axis=0).astype(jnp.bfloat16)
                q_in = (q * jnp.exp(b)).astype(jnp.bfloat16)
                state_scr[h] = chunk_update(slot, rows, h, state_scr[h] * keep_c,
                                            a, q_in, k, v, b)
            return carry

        lax.fori_loop(0, nchunk, chunk_body, 0)

    def output_stage(slot):
        parts = []
        for h in range(GLA_HEADS):
            vs = slice(h * GLA_DV, (h + 1) * GLA_DV)
            o = o_scr[slot, :, vs].astype(jnp.float32)
            r = r_ref[0, :, vs].astype(jnp.float32)
            y = o * _rms_scale(o) * go_ref[:, vs]
            parts.append((y * (r / (1.0 + jnp.exp(-r)))).astype(jnp.bfloat16))
        y = jnp.concatenate(parts, axis=-1)
        out_ref[0] = x_ref[0] + jnp.dot(y, wo_ref[...], preferred_element_type=jnp.float32)

    def step(cur, nxt):
        keep = jnp.where((t - 1) % blocks_per_seq == 0, 0.0, 1.0)
        mild = decay_scr[cur] <= GLA_FAST_MAX_DECAY

        @pl.when(mild)
        def _():
            gates(nxt)
            fast_path(cur, keep)
            output_stage(nxt)

        @pl.when(jnp.logical_not(mild))
        def _():
            gates(nxt)
            robust_path(cur, keep)
            output_stage(nxt)

    @pl.when(t == 0)
    def _():
        state_scr[...] = jnp.zeros_like(state_scr)
        o_scr[...] = jnp.zeros_like(o_scr)
        gates(0)

    @pl.when((t > 0) & (t % 2 == 1))
    def _():
        step(0, 1)

    @pl.when((t > 0) & (t % 2 == 0))
    def _():
        step(1, 0)


def _gla_core(qkvr, z, w_gate, b_gate, g_out, w_out, x3d, *, layer):
    batch, seq, d = x3d.shape
    hk = GLA_HEADS * GLA_DK
    hv = GLA_HEADS * GLA_DV
    tb = GLA_BLOCK
    bps = seq // tb
    n_blocks = batch * bps

    def lagged(lag, col):
        def index_map(t):
            blk = jnp.clip(t - lag, 0, n_blocks - 1)
            return (blk // bps, blk % bps, col)
        return index_map

    return pl.pallas_call(
        functools.partial(_gla_kernel, blocks_per_seq=bps),
        grid=(n_blocks + 2,),
        in_specs=[
            pl.BlockSpec((1, tb, hk), lagged(1, 0)),
            pl.BlockSpec((1, tb, hk), lagged(1, 1)),
            pl.BlockSpec((1, tb, hv), lagged(1, 1)),
            pl.BlockSpec((1, tb, LANES), lagged(0, 0)),
            pl.BlockSpec((LANES, hk), lambda t: (0, 0)),
            pl.BlockSpec((1, hk), lambda t: (0, 0)),
            pl.BlockSpec((1, tb, hv), lagged(2, 2)),
            pl.BlockSpec((1, hv), lambda t: (0, 0)),
            pl.BlockSpec((None, hv, d), lambda t: (layer, 0, 0)),
            pl.BlockSpec((1, tb, d), lagged(2, 0)),
        ],
        out_specs=pl.BlockSpec((1, tb, d), lagged(2, 0)),
        out_shape=jax.ShapeDtypeStruct((batch, seq, d), jnp.float32),
        scratch_shapes=[
            pltpu.VMEM((GLA_HEADS, GLA_DV, GLA_DK), jnp.float32),
            pltpu.VMEM((2, GLA_HEADS, GLA_BLOCK, GLA_DK), jnp.float32),
            pltpu.VMEM((GLA_HEADS, GLA_CHUNK, GLA_DK), jnp.float32),
            pltpu.VMEM((GLA_HEADS, GLA_CHUNK, GLA_DK), jnp.float32),
            pltpu.VMEM((2, GLA_BLOCK, hv), jnp.bfloat16),
            pltpu.SMEM((2,), jnp.float32),
        ],
        compiler_params=_cparams(("arbitrary",)),
        name="gla_core",
    )(qkvr, qkvr, qkvr, z, w_gate, b_gate, qkvr, g_out, w_out, x3d)


def _swa_kernel(q_ref, k_ref, v_ref, o_ref, st_ref, kprev_scr, vprev_scr, *, nsub):
    blk = SWA_SPAN
    first = pl.program_id(2) == 0

    @pl.when(first)
    def _():
        kprev_scr[...] = jnp.zeros_like(kprev_scr)
        vprev_scr[...] = jnp.zeros_like(vprev_scr)

    qi = lax.broadcasted_iota(jnp.int32, (blk, 2 * blk), 0)
    kc = lax.broadcasted_iota(jnp.int32, (blk, 2 * blk), 1)
    band = (kc >= qi) & (kc <= qi + blk)
    lo = jnp.where(first, blk, 0)
    band_first = (kc >= jnp.maximum(qi, lo)) & (kc <= qi + blk)
    lane = lax.broadcasted_iota(jnp.int32, (blk, LANES), 1)
    for r in range(q_ref.shape[1]):
        for s in range(nsub):
            rows = slice(s * blk, (s + 1) * blk)
            valid = band_first if s == 0 else band
            m_tile = jnp.zeros((blk, LANES), jnp.float32)
            l_tile = jnp.ones((blk, LANES), jnp.float32)
            for h in range(SWA_HEADS):
                hs = slice(h * SWA_HEAD_DIM, (h + 1) * SWA_HEAD_DIM)
                q = q_ref[0, r, rows, hs]
                if s == 0:
                    k = jnp.concatenate([kprev_scr[r, :, hs], k_ref[0, r, rows, hs]], axis=0)
                    v = jnp.concatenate([vprev_scr[r, :, hs], v_ref[0, r, rows, hs]], axis=0)
                else:
                    k = k_ref[0, r, (s - 1) * blk:(s + 1) * blk, hs]
                    v = v_ref[0, r, (s - 1) * blk:(s + 1) * blk, hs]
                sc = jnp.where(valid, _dot_nt(q, k), -jnp.inf)
                m = jnp.max(sc, axis=-1, keepdims=True)
                p = jnp.exp2(sc - m)
                l = jnp.sum(p, axis=-1, keepdims=True)
                o = jnp.dot(p.astype(jnp.bfloat16), v, preferred_element_type=jnp.float32)
                o_ref[0, r, rows, hs] = o.astype(o_ref.dtype)
                m_tile = jnp.where(lane == h, m, m_tile)
                l_tile = jnp.where(lane == h, l, l_tile)
            st_ref[0, r, rows, :LANES] = m_tile
            st_ref[0, r, rows, LANES:] = l_tile
        kprev_scr[r] = k_ref[0, r, (nsub - 1) * blk:nsub * blk, :]
        vprev_scr[r] = v_ref[0, r, (nsub - 1) * blk:nsub * blk, :]


def _swa_group(qkv, *, dil, rows, subseqs):
    batch, _, sub_len, _ = qkv.shape
    hd = SWA_HEADS * SWA_HEAD_DIM
    col = lambda c: (lambda b, r, j: (b, r, j, c))
    return pl.pallas_call(
        functools.partial(_swa_kernel, nsub=rows // SWA_SPAN),
        grid=(batch, dil // subseqs, sub_len // rows),
        in_specs=[pl.BlockSpec((1, subseqs, rows, hd), col(0)),
                  pl.BlockSpec((1, subseqs, rows, hd), col(1)),
                  pl.BlockSpec((1, subseqs, rows, hd), col(2))],
        out_specs=[pl.BlockSpec((1, subseqs, rows, hd), col(0)),
                   pl.BlockSpec((1, subseqs, rows, 2 * LANES), col(0))],
        out_shape=[jax.ShapeDtypeStruct((batch, dil, sub_len, hd), jnp.bfloat16),
                   jax.ShapeDtypeStruct((batch, dil, sub_len, 2 * LANES), jnp.float32)],
        scratch_shapes=[pltpu.VMEM((subseqs, SWA_SPAN, hd), jnp.bfloat16),
                        pltpu.VMEM((subseqs, SWA_SPAN, hd), jnp.bfloat16)],
        compiler_params=_cparams(("parallel", "parallel", "arbitrary")),
        name=f"swa_attn_d{dil}",
    )(qkv, qkv, qkv)


def _swa_out_kernel(o1_ref, o2_ref, o3_ref, l1_ref, l2_ref, l3_ref, w_ref, x_ref, out_ref):
    tm = x_ref.shape[1]

    def rows_of(ref, dil, sb):
        n = PERM_ROWS // dil
        if dil == 1:
            return ref[0, 0, sb * PERM_ROWS:(sb + 1) * PERM_ROWS, :]
        return jnp.concatenate([ref[0, r, sb * n:(sb + 1) * n, :] for r in range(dil)], axis=0)

    def natural_bf16(ref, dil, sb):
        rm = rows_of(ref, dil, sb)
        if dil == 1:
            return rm.astype(jnp.float32)
        return jnp.dot(_perm_matrix(dil, to_natural=True), rm,
                       preferred_element_type=jnp.float32)

    def natural_f32(ref, dil, sb):
        rm = rows_of(ref, dil, sb)
        if dil == 1:
            return rm
        perm = _perm_matrix(dil, to_natural=True)
        hi = rm.astype(jnp.bfloat16)
        lo = (rm - hi.astype(jnp.float32)).astype(jnp.bfloat16)
        return (jnp.dot(perm, hi, preferred_element_type=jnp.float32)
                + jnp.dot(perm, lo, preferred_element_type=jnp.float32))

    dils = [d for _, d in SWA_PATTERNS]
    merged = []
    for sb in range(tm // PERM_ROWS):
        stats = [natural_f32(r, d, sb) for r, d in zip((l1_ref, l2_ref, l3_ref), dils)]
        o1, o2, o3 = (natural_bf16(r, d, sb) for r, d in zip((o1_ref, o2_ref, o3_ref), dils))
        sums = [st[:, LANES:] for st in stats]
        lse = [st[:, :LANES] + jnp.log2(l) for st, l in zip(stats, sums)]
        mx = jnp.maximum(jnp.maximum(lse[0], lse[1]), lse[2])
        e = [jnp.exp2(v - mx) for v in lse]
        inv = 1.0 / (e[0] + e[1] + e[2])
        c1, c2, c3 = (eg * inv / l for eg, l in zip(e, sums))
        parts = []
        for h in range(SWA_HEADS):
            hs = slice(h * SWA_HEAD_DIM, (h + 1) * SWA_HEAD_DIM)
            parts.append((c1[:, h:h + 1] * o1[:, hs] + c2[:, h:h + 1] * o2[:, hs]
                          + c3[:, h:h + 1] * o3[:, hs]).astype(jnp.bfloat16))
        merged.append(jnp.concatenate(parts, axis=-1))
    o = jnp.concatenate(merged, axis=0)
    out_ref[0] = x_ref[0] + jnp.dot(o, w_ref[...], preferred_element_type=jnp.float32)


def _swa_out(outs, lses, w, x3d, *, layer, tm):
    batch, seq, d = x3d.shape
    hd = w.shape[1]
    assert tm % PERM_ROWS == 0

    def group_tile(dil, n):
        return pl.BlockSpec((1, dil, tm // dil, n), lambda b, i: (b, 0, i, 0))

    dils = [dil for _, dil in SWA_PATTERNS]
    return pl.pallas_call(
        _swa_out_kernel,
        grid=(batch, seq // tm),
        in_specs=[group_tile(dil, hd) for dil in dils] + [group_tile(dil, 2 * LANES) for dil in dils]
        + [pl.BlockSpec((None, hd, d), lambda b, i: (layer, 0, 0)),
           pl.BlockSpec((1, tm, d), lambda b, i: (b, i, 0))],
        out_specs=pl.BlockSpec((1, tm, d), lambda b, i: (b, i, 0)),
        out_shape=jax.ShapeDtypeStruct((batch, seq, d), jnp.float32),
        compiler_params=_cparams(("parallel", "parallel")),
        name="swa_merge_out",
    )(*outs, *lses, w, x3d)


def kernel(x, norm_mix, norm_mlp, gla_w_in, gla_w_gate_up, gla_b_gate, gla_g_out, gla_w_out,
           swa_w_qkv, swa_g_q, swa_g_k, swa_w_out, mlp_w_up, mlp_w_down):
    batch, seq, d = x.shape
    depth = norm_mix.shape[0]
    hk = GLA_HEADS * GLA_DK
    hv = GLA_HEADS * GLA_DV
    n_main = 2 * hk + 2 * hv
    bf = jnp.bfloat16
    assert seq % GLA_BLOCK == 0
    for window, dilation in SWA_PATTERNS:
        assert seq % window == 0

    gla_w_out_bf, swa_w_qkv_bf, swa_w_out_bf = (w.astype(bf) for w in
                                                (gla_w_out, swa_w_qkv, swa_w_out))
    mlp_w_up_bf, mlp_w_down_bf = mlp_w_up.astype(bf), mlp_w_down.astype(bf)
    gla_w_main_bf = gla_w_in[:, :, :n_main].astype(bf)

    x2d = x.reshape(batch * seq, d)
    for i in range(depth):
        j = i // 2
        if i % 2 == 0:
            w_z = jnp.pad(gla_w_in[j, :, n_main:],
                          ((0, 0), (0, LANES - GLA_GATE_RANK))).astype(bf)
            qkvr, z = _gla_in_proj(x2d, norm_mix[i], gla_w_main_bf, w_z, layer=j, tm=1024)
            w_gate = jnp.pad(gla_w_gate_up[j], ((0, LANES - GLA_GATE_RANK), (0, 0))).astype(bf)
            x2d = _gla_core(qkvr.reshape(batch, seq, n_main), z.reshape(batch, seq, LANES),
                            w_gate, gla_b_gate[j].reshape(1, hk), gla_g_out[j].reshape(1, hv),
                            gla_w_out_bf, x2d.reshape(batch, seq, d),
                            layer=j).reshape(batch * seq, d)
        else:
            outs, lses = [], []
            for gi, (window, dil) in enumerate(SWA_PATTERNS):
                assert window // dil == SWA_SPAN
                gains = jnp.stack([swa_g_q[j, gi] * (SWA_HEAD_DIM ** -0.5 * LOG2E), swa_g_k[j, gi]])
                qkv = _swa_proj(x2d, norm_mix[i], swa_w_qkv_bf, gains, layer=j, batch=batch,
                                seq=seq, group=gi, dil=dil, tm=1024)
                rows = min(SWA_ATTN_ROWS, seq // dil)
                o, lse = _swa_group(qkv, dil=dil, rows=rows, subseqs=SWA_ATTN_ROWS // rows)
                outs.append(o)
                lses.append(lse)
            x2d = _swa_out(outs, lses, swa_w_out_bf, x2d.reshape(batch, seq, d),
                           layer=j, tm=1024).reshape(batch * seq, d)
        x2d = _mlp(x2d, norm_mlp[i], mlp_w_up_bf, mlp_w_down_bf, layer=i, tm=1024)
    return x2d.reshape(batch, seq, d)
```

```python
import functools

import jax
import jax.numpy as jnp
from jax import lax
from jax.experimental import pallas as pl
from jax.experimental.pallas import tpu as pltpu

EPS = 1e-6
LOG2E = 1.4426950408889634
LN2 = 0.6931471805599453

GLA_HEADS = 4
GLA_DK = 128
GLA_DV = 256
GLA_GATE_RANK = 16
GLA_GATE_TAU = 16.0
GLA_CHUNK = 64
GLA_SUB = 16
GLA_BLOCK = 512
GLA_FAST_CHUNK = 256
GLA_FAST_MAX_DECAY = 40.0

SWA_PATTERNS = ((128, 1), (512, 4), (2048, 16))
SWA_HEADS = 8
SWA_HEAD_DIM = 128
SWA_SPAN = 128
PERM_ROWS = 256
SWA_ATTN_ROWS = 2048

LANES = 128
VMEM_LIMIT_BYTES = 56 * 1024 * 1024


def _cparams(semantics):
    return pltpu.CompilerParams(dimension_semantics=semantics,
                                vmem_limit_bytes=VMEM_LIMIT_BYTES)


def _rms_scale(x32):
    return lax.rsqrt(jnp.mean(x32 * x32, axis=-1, keepdims=True) + EPS)


def _perm_matrix(dil, to_natural):
    n = PERM_ROWS // dil
    row = lax.broadcasted_iota(jnp.int32, (PERM_ROWS, PERM_ROWS), 0)
    col = lax.broadcasted_iota(jnp.int32, (PERM_ROWS, PERM_ROWS), 1)
    tok, rm = (row, col) if to_natural else (col, row)
    rm_of_tok = (tok & (dil - 1)) * n + (tok >> (dil.bit_length() - 1))
    return jnp.where(rm == rm_of_tok, 1.0, 0.0).astype(jnp.bfloat16)


def _norm_rows(x32, g):
    return (x32 * _rms_scale(x32) * g).astype(jnp.bfloat16)


def _pipelined_tiles(norm_into, project_from):
    t = pl.program_id(0)

    @pl.when(t == 0)
    def _():
        norm_into(0)

    @pl.when((t > 0) & (t % 2 == 1))
    def _():
        norm_into(1)
        project_from(0)

    @pl.when((t > 0) & (t % 2 == 0))
    def _():
        norm_into(0)
        project_from(1)


def _swa_proj_kernel(x_ref, g_ref, w_ref, gqk_ref, o_ref, h_scr, *, dil):
    tm = x_ref.shape[0]
    n = PERM_ROWS // dil
    hd = SWA_HEADS * SWA_HEAD_DIM

    def norm_into(slot):
        h = _norm_rows(x_ref[...], g_ref[...])
        if dil == 1:
            h_scr[slot] = h
        else:
            perm = _perm_matrix(dil, to_natural=False)
            for sb in range(tm // PERM_ROWS):
                rows = slice(sb * PERM_ROWS, (sb + 1) * PERM_ROWS)
                h_scr[slot, rows, :] = jnp.dot(
                    perm, h[rows], preferred_element_type=jnp.float32).astype(jnp.bfloat16)

    def store(y_bf16, cols):
        if dil == 1:
            o_ref[0, 0, :, cols] = y_bf16
        else:
            for sb in range(tm // PERM_ROWS):
                for r in range(dil):
                    src = sb * PERM_ROWS + r * n
                    o_ref[0, r, sb * n:(sb + 1) * n, cols] = y_bf16[src:src + n]

    def project_from(slot):
        for j in range(3):
            y = jnp.dot(h_scr[slot], w_ref[:, j * hd:(j + 1) * hd],
                        preferred_element_type=jnp.float32)
            if j == 2:
                store(y.astype(jnp.bfloat16), slice(j * hd, (j + 1) * hd))
                continue
            gain = gqk_ref[j:j + 1, :]
            for h in range(SWA_HEADS):
                yh = y[:, h * SWA_HEAD_DIM:(h + 1) * SWA_HEAD_DIM]
                store((yh * _rms_scale(yh) * gain).astype(jnp.bfloat16),
                      slice(j * hd + h * SWA_HEAD_DIM, j * hd + (h + 1) * SWA_HEAD_DIM))

    _pipelined_tiles(norm_into, project_from)


def _swa_proj(x2d, g, w_qkv, gains_qk, *, layer, batch, seq, group, dil, tm):
    d = x2d.shape[1]
    gcols = 3 * SWA_HEADS * SWA_HEAD_DIM
    tiles_per_seq = seq // tm
    n_tiles = batch * tiles_per_seq

    def out_map(t):
        tile = jnp.maximum(t - 1, 0)
        return (tile // tiles_per_seq, 0, tile % tiles_per_seq, 0)

    return pl.pallas_call(
        functools.partial(_swa_proj_kernel, dil=dil),
        grid=(n_tiles + 1,),
        in_specs=[
            pl.BlockSpec((tm, d), lambda t: (jnp.minimum(t, n_tiles - 1), 0)),
            pl.BlockSpec((1, d), lambda t: (0, 0)),
            pl.BlockSpec((None, d, gcols), lambda t: (layer, 0, group)),
            pl.BlockSpec((2, SWA_HEAD_DIM), lambda t: (0, 0)),
        ],
        out_specs=pl.BlockSpec((1, dil, tm // dil, gcols), out_map),
        out_shape=jax.ShapeDtypeStruct((batch, dil, seq // dil, gcols), jnp.bfloat16),
        scratch_shapes=[pltpu.VMEM((2, tm, d), jnp.bfloat16)],
        compiler_params=_cparams(("arbitrary",)),
        name=f"swa_proj_d{dil}",
    )(x2d, g.reshape(1, d), w_qkv, gains_qk)


def _gla_in_kernel(x_ref, g_ref, w_ref, wz_ref, o_ref, z_ref, h_scr):
    def norm_into(slot):
        h_scr[slot] = _norm_rows(x_ref[...], g_ref[...])

    def project_from(slot):
        h = h_scr[slot]
        z_ref[...] = jnp.dot(h, wz_ref[...], preferred_element_type=jnp.float32)
        tn = GLA_HEADS * GLA_DV
        for j in range(w_ref.shape[1] // tn):
            cols = slice(j * tn, (j + 1) * tn)
            o_ref[:, cols] = jnp.dot(h, w_ref[:, cols],
                                     preferred_element_type=jnp.float32).astype(o_ref.dtype)

    _pipelined_tiles(norm_into, project_from)


def _gla_in_proj(x2d, g, w_main, w_z, *, layer, tm):
    m, d = x2d.shape
    n = w_main.shape[2]
    nz = w_z.shape[1]
    n_tiles = m // tm
    cur = lambda t: (jnp.maximum(t - 1, 0), 0)
    return pl.pallas_call(
        _gla_in_kernel,
        grid=(n_tiles + 1,),
        in_specs=[
            pl.BlockSpec((tm, d), lambda t: (jnp.minimum(t, n_tiles - 1), 0)),
            pl.BlockSpec((1, d), lambda t: (0, 0)),
            pl.BlockSpec((None, d, n), lambda t: (layer, 0, 0)),
            pl.BlockSpec((d, nz), lambda t: (0, 0)),
        ],
        out_specs=[pl.BlockSpec((tm, n), cur), pl.BlockSpec((tm, nz), cur)],
        out_shape=[
            jax.ShapeDtypeStruct((m, n), jnp.bfloat16),
            jax.ShapeDtypeStruct((m, nz), jnp.float32),
        ],
        scratch_shapes=[pltpu.VMEM((2, tm, d), jnp.bfloat16)],
        compiler_params=_cparams(("arbitrary",)),
        name="gla_in_proj",
    )(x2d, g.reshape(1, d), w_main, w_z)


def _mlp_kernel(x_ref, xn_ref, g_ref, wu_ref, wd_ref, out_ref, h_scr):
    @pl.when(pl.program_id(0) == 0)
    def _():
        h_scr[...] = _norm_rows(x_ref[...], g_ref[...])

    u = jnp.dot(h_scr[...], wu_ref[...], preferred_element_type=jnp.float32)
    a = jnp.square(jnp.maximum(u, 0.0)).astype(jnp.bfloat16)
    h_scr[...] = _norm_rows(xn_ref[...], g_ref[...])
    out_ref[...] = x_ref[...] + jnp.dot(a, wd_ref[...], preferred_element_type=jnp.float32)


def _mlp(x2d, g, w_up, w_down, *, layer, tm):
    m, d = x2d.shape
    ff = w_up.shape[2]
    n_tiles = m // tm
    resident = pl.Buffered(1)
    return pl.pallas_call(
        _mlp_kernel,
        grid=(n_tiles,),
        in_specs=[
            pl.BlockSpec((tm, d), lambda i: (i, 0)),
            pl.BlockSpec((tm, d), lambda i: (jnp.minimum(i + 1, n_tiles - 1), 0)),
            pl.BlockSpec((1, d), lambda i: (0, 0)),
            pl.BlockSpec((None, d, ff), lambda i: (layer, 0, 0), pipeline_mode=resident),
            pl.BlockSpec((None, ff, d), lambda i: (layer, 0, 0), pipeline_mode=resident),
        ],
        out_specs=pl.BlockSpec((tm, d), lambda i: (i, 0)),
        out_shape=jax.ShapeDtypeStruct((m, d), jnp.float32),
        scratch_shapes=[pltpu.VMEM((tm, d), jnp.bfloat16)],
        compiler_params=_cparams(("arbitrary",)),
        name="sqrelu_mlp",
    )(x2d, x2d, g.reshape(1, d), w_up, w_down)


def _dot_nt(a, b):
    return lax.dot_general(a, b, (((1,), (1,)), ((), ())),
                           preferred_element_type=jnp.float32)


def _dot_tn(a, b):
    return lax.dot_general(a, b, (((0,), (0,)), ((), ())),
                           preferred_element_type=jnp.float32)


def _gla_kernel(q_ref, k_ref, v_ref, z_ref, wg_ref, bg_ref, r_ref, go_ref, wo_ref, x_ref,
                out_ref, state_scr, b_scr, k_scr, brel_scr, o_scr, decay_scr, *, blocks_per_seq):
    C, SUB, CF = GLA_CHUNK, GLA_SUB, GLA_FAST_CHUNK
    nsub = C // SUB
    nchunk = GLA_BLOCK // C
    t = pl.program_id(0)

    row = lax.broadcasted_iota(jnp.int32, (CF, CF), 0)
    col = lax.broadcasted_iota(jnp.int32, (CF, CF), 1)
    causal = row >= col
    tri = jnp.where(causal, 1.0, 0.0).astype(jnp.bfloat16)

    def gates(slot):
        zb = z_ref[0].astype(jnp.bfloat16)
        gp = jnp.dot(zb, wg_ref[...], preferred_element_type=jnp.float32) + bg_ref[...]
        g = (jnp.minimum(gp, 0.0) - jnp.log(1.0 + jnp.exp(-jnp.abs(gp)))) * (1.0 / GLA_GATE_TAU)
        g_hi = g.astype(jnp.bfloat16)
        g_lo = (g - g_hi.astype(jnp.float32)).astype(jnp.bfloat16)
        decay_max = jnp.zeros((1, GLA_DK), jnp.float32)
        for h in range(GLA_HEADS):
            ks = slice(h * GLA_DK, (h + 1) * GLA_DK)
            for cf in range(GLA_BLOCK // CF):
                rows = slice(cf * CF, (cf + 1) * CF)
                g_hl = jnp.concatenate([g_hi[rows, ks], g_lo[rows, ks]], axis=-1)
                b_hl = jnp.dot(tri, g_hl, preferred_element_type=jnp.float32)
                b = b_hl[:, :GLA_DK] + b_hl[:, GLA_DK:]
                b_scr[slot, h, rows, :] = b
                decay_max = jnp.maximum(decay_max, -b[CF - 1:CF, :])
        decay_scr[slot] = jnp.max(decay_max)

    def chunk_update(slot, ci_rows, h, st, a, q_in, k, v, b):
        vs = slice(h * GLA_DV, (h + 1) * GLA_DV)
        b_last = b[b.shape[0] - 1:, :]
        o = (jnp.dot(a, v, preferred_element_type=jnp.float32)
             + _dot_nt(q_in, st.astype(jnp.bfloat16)))
        o_scr[slot, ci_rows, vs] = o.astype(o_scr.dtype)
        k_dec = (k * jnp.exp(b_last - b)).astype(jnp.bfloat16)
        return st * jnp.exp(b_last) + _dot_tn(v, k_dec)

    def fast_path(slot, keep):
        for cf in range(GLA_BLOCK // CF):
            rows = slice(cf * CF, (cf + 1) * CF)
            for h in range(GLA_HEADS):
                ks = slice(h * GLA_DK, (h + 1) * GLA_DK)
                vs = slice(h * GLA_DV, (h + 1) * GLA_DV)
                b = b_scr[slot, h, rows, :]
                q = q_ref[0, rows, ks].astype(jnp.float32) * (GLA_DK ** -0.5)
                k = k_ref[0, rows, ks].astype(jnp.float32)
                v = v_ref[0, rows, vs]
                q_in = (q * jnp.exp(b)).astype(jnp.bfloat16)
                k_out = (k * jnp.exp(-b)).astype(jnp.bfloat16)
                a = jnp.where(causal, _dot_nt(q_in, k_out), 0.0).astype(jnp.bfloat16)
                st = state_scr[h] * keep if cf == 0 else state_scr[h]
                state_scr[h] = chunk_update(slot, rows, h, st, a, q_in, k, v, b)

    def robust_path(slot, keep):
        srow = lax.broadcasted_iota(jnp.int32, (SUB, C), 0)
        scol = lax.broadcasted_iota(jnp.int32, (SUB, C), 1)

        def chunk_body(ci, carry):
            rows = pl.ds(pl.multiple_of(ci * C, C), C)
            keep_c = jnp.where(ci == 0, keep, 1.0)
            for h in range(GLA_HEADS):
                ks = slice(h * GLA_DK, (h + 1) * GLA_DK)
                vs = slice(h * GLA_DV, (h + 1) * GLA_DV)
                c0 = ci * C
                before = b_scr[slot, h, pl.ds(jnp.maximum(c0 - 1, 0), 1), :]
                base = jnp.where(c0 % CF == 0, 0.0, before)
                b = b_scr[slot, h, rows, :] - base
                q = q_ref[0, rows, ks].astype(jnp.float32) * (GLA_DK ** -0.5)
                k = k_ref[0, rows, ks].astype(jnp.float32)
                v = v_ref[0, rows, vs]
                k_scr[h] = k
                brel_scr[h] = b

                refs = [jnp.zeros((1, GLA_DK), jnp.float32)]
                for i in range(1, nsub):
                    refs.append(brel_scr[h, i * SUB - 1:i * SUB, :])
                ref_rows = jnp.concatenate(
                    [jnp.broadcast_to(r, (SUB, GLA_DK)) for r in refs], axis=0)
                qe = (q * jnp.exp(b - ref_rows)).astype(jnp.bfloat16)

                a_rows = []
                for i in range(nsub):
                    r0 = i * SUB
                    qi = q[r0:r0 + SUB]
                    bi = b[r0:r0 + SUB]
                    key = jnp.where(srow >= scol - r0, scol, -1)
                    blk = jnp.zeros((SUB, C), jnp.float32)
                    for jj in range(SUB):
                        j = r0 + jj
                        bj = brel_scr[h, j:j + 1, :]
                        kj = k_scr[h, j:j + 1, :]
                        e = jnp.exp(jnp.minimum(bi - bj, 0.0))
                        s = jnp.sum(qi * e * kj, axis=-1, keepdims=True)
                        blk = jnp.where(key == j, s, blk)
                    if i > 0:
                        ke = (k * jnp.exp(jnp.minimum(refs[i] - b, 0.0))).astype(jnp.bfloat16)
                        off = _dot_nt(qe[r0:r0 + SUB], ke)
                        blk = jnp.where(scol < r0, off, blk)
                    a_rows.append(blk)
                a = jnp.concatenate(a_rows, axis=0).astype(jnp.bfloat16)
                q_in = (q * jnp.exp(b)).astype(jnp.bfloat16)
                state_scr[h] = chunk_update(slot, rows, h, state_scr[h] * keep_c,
                                            a, q_in, k, v, b)
            return carry

        lax.fori_loop(0, nchunk, chunk_body, 0)

    def output_stage(slot):
        parts = []
        for h in range(GLA_HEADS):
            vs = slice(h * GLA_DV, (h + 1) * GLA_DV)
            o = o_scr[slot, :, vs].astype(jnp.float32)
            r = r_ref[0, :, vs].astype(jnp.float32)
            y = o * _rms_scale(o) * go_ref[:, vs]
            parts.append((y * (r / (1.0 + jnp.exp(-r)))).astype(jnp.bfloat16))
        y = jnp.concatenate(parts, axis=-1)
        out_ref[0] = x_ref[0] + jnp.dot(y, wo_ref[...], preferred_element_type=jnp.float32)

    def step(cur, nxt):
        keep = jnp.where((t - 1) % blocks_per_seq == 0, 0.0, 1.0)
        mild = decay_scr[cur] <= GLA_FAST_MAX_DECAY

        @pl.when(mild)
        def _():
            gates(nxt)
            fast_path(cur, keep)
            output_stage(nxt)

        @pl.when(jnp.logical_not(mild))
        def _():
            gates(nxt)
            robust_path(cur, keep)
            output_stage(nxt)

    @pl.when(t == 0)
    def _():
        state_scr[...] = jnp.zeros_like(state_scr)
        o_scr[...] = jnp.zeros_like(o_scr)
        gates(0)

    @pl.when((t > 0) & (t % 2 == 1))
    def _():
        step(0, 1)

    @pl.when((t > 0) & (t % 2 == 0))
    def _():
        step(1, 0)


def _gla_core(qkvr, z, w_gate, b_gate, g_out, w_out, x3d, *, layer):
    batch, seq, d = x3d.shape
    hk = GLA_HEADS * GLA_DK
    hv = GLA_HEADS * GLA_DV
    tb = GLA_BLOCK
    bps = seq // tb
    n_blocks = batch * bps

    def lagged(lag, col):
        def index_map(t):
            blk = jnp.clip(t - lag, 0, n_blocks - 1)
            return (blk // bps, blk % bps, col)
        return index_map

    return pl.pallas_call(
        functools.partial(_gla_kernel, blocks_per_seq=bps),
        grid=(n_blocks + 2,),
        in_specs=[
            pl.BlockSpec((1, tb, hk), lagged(1, 0)),
            pl.BlockSpec((1, tb, hk), lagged(1, 1)),
            pl.BlockSpec((1, tb, hv), lagged(1, 1)),
            pl.BlockSpec((1, tb, LANES), lagged(0, 0)),
            pl.BlockSpec((LANES, hk), lambda t: (0, 0)),
            pl.BlockSpec((1, hk), lambda t: (0, 0)),
            pl.BlockSpec((1, tb, hv), lagged(2, 2)),
            pl.BlockSpec((1, hv), lambda t: (0, 0)),
            pl.BlockSpec((None, hv, d), lambda t: (layer, 0, 0)),
            pl.BlockSpec((1, tb, d), lagged(2, 0)),
        ],
        out_specs=pl.BlockSpec((1, tb, d), lagged(2, 0)),
        out_shape=jax.ShapeDtypeStruct((batch, seq, d), jnp.float32),
        scratch_shapes=[
            pltpu.VMEM((GLA_HEADS, GLA_DV, GLA_DK), jnp.float32),
            pltpu.VMEM((2, GLA_HEADS, GLA_BLOCK, GLA_DK), jnp.float32),
            pltpu.VMEM((GLA_HEADS, GLA_CHUNK, GLA_DK), jnp.float32),
            pltpu.VMEM((GLA_HEADS, GLA_CHUNK, GLA_DK), jnp.float32),
            pltpu.VMEM((2, GLA_BLOCK, hv), jnp.bfloat16),
            pltpu.SMEM((2,), jnp.float32),
        ],
        compiler_params=_cparams(("arbitrary",)),
        name="gla_core",
    )(qkvr, qkvr, qkvr, z, w_gate, b_gate, qkvr, g_out, w_out, x3d)


def _swa_kernel(q_ref, k_ref, v_ref, o_ref, st_ref, kprev_scr, vprev_scr, *, nsub):
    blk = SWA_SPAN
    first = pl.program_id(2) == 0

    @pl.when(first)
    def _():
        kprev_scr[...] = jnp.zeros_like(kprev_scr)
        vprev_scr[...] = jnp.zeros_like(vprev_scr)

    qi = lax.broadcasted_iota(jnp.int32, (blk, 2 * blk), 0)
    kc = lax.broadcasted_iota(jnp.int32, (blk, 2 * blk), 1)
    band = (kc >= qi) & (kc <= qi + blk)
    lo = jnp.where(first, blk, 0)
    band_first = (kc >= jnp.maximum(qi, lo)) & (kc <= qi + blk)
    lane = lax.broadcasted_iota(jnp.int32, (blk, LANES), 1)
    for r in range(q_ref.shape[1]):
        for s in range(nsub):
            rows = slice(s * blk, (s + 1) * blk)
            valid = band_first if s == 0 else band
            m_tile = jnp.zeros((blk, LANES), jnp.float32)
            l_tile = jnp.ones((blk, LANES), jnp.float32)
            for h in range(SWA_HEADS):
                hs = slice(h * SWA_HEAD_DIM, (h + 1) * SWA_HEAD_DIM)
                q = q_ref[0, r, rows, hs]
                if s == 0:
                    k = jnp.concatenate([kprev_scr[r, :, hs], k_ref[0, r, rows, hs]], axis=0)
                    v = jnp.concatenate([vprev_scr[r, :, hs], v_ref[0, r, rows, hs]], axis=0)
                else:
                    k = k_ref[0, r, (s - 1) * blk:(s + 1) * blk, hs]
                    v = v_ref[0, r, (s - 1) * blk:(s + 1) * blk, hs]
                sc = jnp.where(valid, _dot_nt(q, k), -jnp.inf)
                m = jnp.max(sc, axis=-1, keepdims=True)
                p = jnp.exp2(sc - m)
                l = jnp.sum(p, axis=-1, keepdims=True)
                o = jnp.dot(p.astype(jnp.bfloat16), v, preferred_element_type=jnp.float32)
                o_ref[0, r, rows, hs] = o.astype(o_ref.dtype)
                m_tile = jnp.where(lane == h, m, m_tile)
                l_tile = jnp.where(lane == h, l, l_tile)
            st_ref[0, r, rows, :LANES] = m_tile
            st_ref[0, r, rows, LANES:] = l_tile
        kprev_scr[r] = k_ref[0, r, (nsub - 1) * blk:nsub * blk, :]
        vprev_scr[r] = v_ref[0, r, (nsub - 1) * blk:nsub * blk, :]


def _swa_group(qkv, *, dil, rows, subseqs):
    batch, _, sub_len, _ = qkv.shape
    hd = SWA_HEADS * SWA_HEAD_DIM
    col = lambda c: (lambda b, r, j: (b, r, j, c))
    return pl.pallas_call(
        functools.partial(_swa_kernel, nsub=rows // SWA_SPAN),
        grid=(batch, dil // subseqs, sub_len // rows),
        in_specs=[pl.BlockSpec((1, subseqs, rows, hd), col(0)),
                  pl.BlockSpec((1, subseqs, rows, hd), col(1)),
                  pl.BlockSpec((1, subseqs, rows, hd), col(2))],
        out_specs=[pl.BlockSpec((1, subseqs, rows, hd), col(0)),
                   pl.BlockSpec((1, subseqs, rows, 2 * LANES), col(0))],
        out_shape=[jax.ShapeDtypeStruct((batch, dil, sub_len, hd), jnp.bfloat16),
                   jax.ShapeDtypeStruct((batch, dil, sub_len, 2 * LANES), jnp.float32)],
        scratch_shapes=[pltpu.VMEM((subseqs, SWA_SPAN, hd), jnp.bfloat16),
                        pltpu.VMEM((subseqs, SWA_SPAN, hd), jnp.bfloat16)],
        compiler_params=_cparams(("parallel", "parallel", "arbitrary")),
        name=f"swa_attn_d{dil}",
    )(qkv, qkv, qkv)


def _swa_out_kernel(o1_ref, o2_ref, o3_ref, l1_ref, l2_ref, l3_ref, w_ref, x_ref, out_ref):
    tm = x_ref.shape[1]

    def rows_of(ref, dil, sb):
        n = PERM_ROWS // dil
        if dil == 1:
            return ref[0, 0, sb * PERM_ROWS:(sb + 1) * PERM_ROWS, :]
        return jnp.concatenate([ref[0, r, sb * n:(sb + 1) * n, :] for r in range(dil)], axis=0)

    def natural_bf16(ref, dil, sb):
        rm = rows_of(ref, dil, sb)
        if dil == 1:
            return rm.astype(jnp.float32)
        return jnp.dot(_perm_matrix(dil, to_natural=True), rm,
                       preferred_element_type=jnp.float32)

    def natural_f32(ref, dil, sb):
        rm = rows_of(ref, dil, sb)
        if dil == 1:
            return rm
        perm = _perm_matrix(dil, to_natural=True)
        hi = rm.astype(jnp.bfloat16)
        lo = (rm - hi.astype(jnp.float32)).astype(jnp.bfloat16)
        return (jnp.dot(perm, hi, preferred_element_type=jnp.float32)
                + jnp.dot(perm, lo, preferred_element_type=jnp.float32))

    dils = [d for _, d in SWA_PATTERNS]
    merged = []
    for sb in range(tm // PERM_ROWS):
        stats = [natural_f32(r, d, sb) for r, d in zip((l1_ref, l2_ref, l3_ref), dils)]
        o1, o2, o3 = (natural_bf16(r, d, sb) for r, d in zip((o1_ref, o2_ref, o3_ref), dils))
        sums = [st[:, LANES:] for st in stats]
        lse = [st[:, :LANES] + jnp.log2(l) for st, l in zip(stats, sums)]
        mx = jnp.maximum(jnp.maximum(lse[0], lse[1]), lse[2])
        e = [jnp.exp2(v - mx) for v in lse]
        inv = 1.0 / (e[0] + e[1] + e[2])
        c1, c2, c3 = (eg * inv / l for eg, l in zip(e, sums))
        parts = []
        for h in range(SWA_HEADS):
            hs = slice(h * SWA_HEAD_DIM, (h + 1) * SWA_HEAD_DIM)
            parts.append((c1[:, h:h + 1] * o1[:, hs] + c2[:, h:h + 1] * o2[:, hs]
                          + c3[:, h:h + 1] * o3[:, hs]).astype(jnp.bfloat16))
        merged.append(jnp.concatenate(parts, axis=-1))
    o = jnp.concatenate(merged, axis=0)
    out_ref[0] = x_ref[0] + jnp.dot(o, w_ref[...], preferred_element_type=jnp.float32)


def _swa_out(outs, lses, w, x3d, *, layer, tm):
    batch, seq, d = x3d.shape
    hd = w.shape[1]
    assert tm % PERM_ROWS == 0

    def group_tile(dil, n):
        return pl.BlockSpec((1, dil, tm // dil, n), lambda b, i: (b, 0, i, 0))

    dils = [dil for _, dil in SWA_PATTERNS]
    return pl.pallas_call(
        _swa_out_kernel,
        grid=(batch, seq // tm),
        in_specs=[group_tile(dil, hd) for dil in dils] + [group_tile(dil, 2 * LANES) for dil in dils]
        + [pl.BlockSpec((None, hd, d), lambda b, i: (layer, 0, 0)),
           pl.BlockSpec((1, tm, d), lambda b, i: (b, i, 0))],
        out_specs=pl.BlockSpec((1, tm, d), lambda b, i: (b, i, 0)),
        out_shape=jax.ShapeDtypeStruct((batch, seq, d), jnp.float32),
        compiler_params=_cparams(("parallel", "parallel")),
        name="swa_merge_out",
    )(*outs, *lses, w, x3d)


def kernel(x, norm_mix, norm_mlp, gla_w_in, gla_w_gate_up, gla_b_gate, gla_g_out, gla_w_out,
           swa_w_qkv, swa_g_q, swa_g_k, swa_w_out, mlp_w_up, mlp_w_down):
    batch, seq, d = x.shape
    depth = norm_mix.shape[0]
    hk = GLA_HEADS * GLA_DK
    hv = GLA_HEADS * GLA_DV
    n_main = 2 * hk + 2 * hv
    bf = jnp.bfloat16
    assert seq % GLA_BLOCK == 0
    for window, dilation in SWA_PATTERNS:
        assert seq % window == 0

    gla_w_out_bf, swa_w_qkv_bf, swa_w_out_bf = (w.astype(bf) for w in
                                                (gla_w_out, swa_w_qkv, swa_w_out))
    mlp_w_up_bf, mlp_w_down_bf = mlp_w_up.astype(bf), mlp_w_down.astype(bf)
    gla_w_main_bf = gla_w_in[:, :, :n_main].astype(bf)

    x2d = x.reshape(batch * seq, d)
    for i in range(depth):
        j = i // 2
        if i % 2 == 0:
            w_z = jnp.pad(gla_w_in[j, :, n_main:],
                          ((0, 0), (0, LANES - GLA_GATE_RANK))).astype(bf)
            qkvr, z = _gla_in_proj(x2d, norm_mix[i], gla_w_main_bf, w_z, layer=j, tm=1024)
            w_gate = jnp.pad(gla_w_gate_up[j], ((0, LANES - GLA_GATE_RANK), (0, 0))).astype(bf)
            x2d = _gla_core(qkvr.reshape(batch, seq, n_main), z.reshape(batch, seq, LANES),
                            w_gate, gla_b_gate[j].reshape(1, hk), gla_g_out[j].reshape(1, hv),
                            gla_w_out_bf, x2d.reshape(batch, seq, d),
                            layer=j).reshape(batch * seq, d)
        else:
            outs, lses = [], []
            for gi, (window, dil) in enumerate(SWA_PATTERNS):
                assert window // dil == SWA_SPAN
                gains = jnp.stack([swa_g_q[j, gi] * (SWA_HEAD_DIM ** -0.5 * LOG2E), swa_g_k[j, gi]])
                qkv = _swa_proj(x2d, norm_mix[i], swa_w_qkv_bf, gains, layer=j, batch=batch,
                                seq=seq, group=gi, dil=dil, tm=1024)
                rows = min(SWA_ATTN_ROWS, seq // dil)
                o, lse = _swa_group(qkv, dil=dil, rows=rows, subseqs=SWA_ATTN_ROWS // rows)
                outs.append(o)
                lses.append(lse)
            x2d = _swa_out(outs, lses, swa_w_out_bf, x2d.reshape(batch, seq, d),
                           layer=j, tm=1024).reshape(batch * seq, d)
        x2d = _mlp(x2d, norm_mlp[i], mlp_w_up_bf, mlp_w_down_bf, layer=i, tm=1024)
    return x2d.reshape(batch, seq, d)
```
